```python
import jax, jax.numpy as jnp
from jax import lax
import numpy as np

D_MODEL = 1024
BATCH = 8
SEQ = 2048
DEPTH = 2
DEC_BATCH = 128
DEC_SEQ = 1
PAST_LEN = 16384
PAGE_SIZE = 128

D_MIX = 2 * D_MODEL
W_A = D_MIX // 2
W_B = D_MIX - W_A
NB_A = 16
BW_A = W_A // NB_A
CONV_W = 4
LRU_C = 8.0
H_B = 4
DK = W_B // H_B
DV = W_B // H_B
CHUNK = 128
EPS = 1e-6
SPLIT_SIZES = (W_A, W_A, W_B, W_B, W_B, W_B, W_B, H_B, H_B)
D_IN = sum(SPLIT_SIZES)
SPLIT_IDX = tuple(int(v) for v in np.cumsum(SPLIT_SIZES)[:-1])

kernel_name = "hymba_rglru_mlstm_decoder_step"


def rmsnorm(x, g):
    xf = x.astype(jnp.float32)
    y = xf * lax.rsqrt(jnp.mean(xf * xf, axis=-1, keepdims=True) + EPS)
    return (y * g.astype(jnp.float32)).astype(x.dtype)


def _lin_combine(e1, e2):
    a1, b1 = e1
    a2, b2 = e2
    return a1 * a2, a2 * b1 + b2


def rglru_branch(xa, conv_buf, h0, conv_w, conv_b, w_r, b_r, w_i, b_i, lam):
    B, S, _ = xa.shape
    ext = jnp.concatenate([conv_buf.astype(xa.dtype), xa], axis=1)
    xc = conv_b.astype(jnp.float32) + sum(
        conv_w[j].astype(jnp.float32) * ext[:, j:j + S].astype(jnp.float32) for j in range(CONV_W))
    new_buf = ext[:, S:]
    blocks = xc.reshape(B, S, NB_A, BW_A)
    r = jax.nn.sigmoid(jnp.einsum('bsnc,ncd->bsnd', blocks, w_r.astype(jnp.float32)).reshape(B, S, W_A)
                       + b_r.astype(jnp.float32))
    i = jax.nn.sigmoid(jnp.einsum('bsnc,ncd->bsnd', blocks, w_i.astype(jnp.float32)).reshape(B, S, W_A)
                       + b_i.astype(jnp.float32))
    log_a = -LRU_C * r * jax.nn.softplus(-lam.astype(jnp.float32))
    a = jnp.exp(log_a)
    u = jnp.sqrt(-jnp.expm1(2.0 * log_a)) * (i * xc)
    A, Bc = lax.associative_scan(_lin_combine, (a, u), axis=1)
    h = A * h0.astype(jnp.float32)[:, None, :] + Bc
    return h, h[:, -1], new_buf


def mlstm_chunk(carry, inp):
    C, n, m = carry
    q, k, v, ig, lf = inp
    L = q.shape[1]
    b = jnp.cumsum(lf, axis=1).transpose(0, 2, 1)
    igh = ig.transpose(0, 2, 1)
    causal = jnp.tril(jnp.ones((L, L), dtype=bool))
    D = jnp.where(causal, b[:, :, :, None] - b[:, :, None, :] + igh[:, :, None, :], -jnp.inf)
    g = b + m[:, :, None]
    m_t = jnp.maximum(g, jnp.max(D, axis=-1))
    w = jnp.exp(D - m_t[..., None])
    ginter = jnp.exp(g - m_t)
    P = w * jnp.einsum('bthk,bshk->bhts', q, k)
    num = (jnp.einsum('bhts,bshv->bthv', P, v)
           + ginter.transpose(0, 2, 1)[..., None] * jnp.einsum('bthk,bhkv->bthv', q, C))
    den = jnp.sum(P, axis=-1) + ginter * jnp.einsum('bthk,bhk->bht', q, n)
    denom = jnp.maximum(jnp.abs(den), jnp.exp(-m_t)).transpose(0, 2, 1)[..., None]
    h = num / denom
    mL = m_t[:, :, -1]
    decay = jnp.exp(b[:, :, -1:] - b + igh - mL[:, :, None])
    carry_scale = jnp.exp(b[:, :, -1] + m - mL)
    C_new = carry_scale[..., None, None] * C + jnp.einsum('bhs,bshk,bshv->bhkv', decay, k, v)
    n_new = carry_scale[..., None] * n + jnp.einsum('bhs,bshk->bhk', decay, k)
    return (C_new, n_new, mL), h


def mlstm_seq(q, k, v, ig, lf, C0, n0, m0):
    B, S = q.shape[:2]
    L = CHUNK if S % CHUNK == 0 else S
    nc = S // L

    def to_chunks(t):
        return jnp.moveaxis(t.reshape((B, nc, L) + t.shape[2:]), 1, 0)

    carry0 = (C0.astype(jnp.float32), n0.astype(jnp.float32), m0.astype(jnp.float32))
    (C, n, m), hs = lax.scan(mlstm_chunk, carry0,
                             (to_chunks(q), to_chunks(k), to_chunks(v), to_chunks(ig), to_chunks(lf)))
    h = jnp.moveaxis(hs, 0, 1).reshape(B, S, H_B, DV)
    return h, C, n, m


def hybrid_layer(x, h0, conv0, C0, n0, m0, g_norm, w_in, conv_w, conv_b, w_rgate, b_rgate,
                 w_igate, b_igate, lru_lambda, b_mi, b_mf, g_mhead, w_out):
    B, S, _ = x.shape
    hn = rmsnorm(x, g_norm)
    u = hn @ w_in
    xa, za, uq, uk, uv, uo, zb, ui, uf = jnp.split(u, SPLIT_IDX, axis=-1)
    yA, hA, bufA = rglru_branch(xa, conv0, h0, conv_w, conv_b, w_rgate, b_rgate, w_igate, b_igate, lru_lambda)
    q = uq.astype(jnp.float32).reshape(B, S, H_B, DK)
    k = uk.astype(jnp.float32).reshape(B, S, H_B, DK) * (DK ** -0.5)
    v = uv.astype(jnp.float32).reshape(B, S, H_B, DV)
    ig = ui.astype(jnp.float32) + b_mi.astype(jnp.float32)
    lf = jax.nn.log_sigmoid(uf.astype(jnp.float32) + b_mf.astype(jnp.float32))
    hB, C, n, m = mlstm_seq(q, k, v, ig, lf, C0, n0, m0)
    yB = jax.nn.sigmoid(uo.astype(jnp.float32)).reshape(B, S, H_B, DV) * hB
    yB = yB * lax.rsqrt(jnp.mean(yB * yB, axis=-1, keepdims=True) + EPS)
    yB = yB.reshape(B, S, W_B) * g_mhead.astype(jnp.float32)
    merged = jnp.concatenate([yA * jax.nn.silu(za.astype(jnp.float32)),
                              yB * jax.nn.silu(zb.astype(jnp.float32))], axis=-1).astype(x.dtype)
    x = x + merged @ w_out
    return x, hA, bufA, C, n, m


def trunk(x, h_s, conv_s, C_s, n_s, m_s, g_norm, w_in, conv_w, conv_b, w_rgate, b_rgate,
          w_igate, b_igate, lru_lambda, b_mi, b_mf, g_mhead, w_out, g_final):
    hs, bufs, Cs, ns, ms = [], [], [], [], []
    for l in range(DEPTH):
        x, hA, bufA, C, n, m = hybrid_layer(
            x, h_s[l], conv_s[l], C_s[l], n_s[l], m_s[l], g_norm[l], w_in[l], conv_w[l], conv_b[l],
            w_rgate[l], b_rgate[l], w_igate[l], b_igate[l], lru_lambda[l], b_mi[l], b_mf[l],
            g_mhead[l], w_out[l])
        hs.append(hA.astype(h_s.dtype)); bufs.append(bufA.astype(conv_s.dtype))
        Cs.append(C.astype(C_s.dtype)); ns.append(n.astype(n_s.dtype)); ms.append(m.astype(m_s.dtype))
    y = rmsnorm(x, g_final)
    return y, jnp.stack(hs), jnp.stack(bufs), jnp.stack(Cs), jnp.stack(ns), jnp.stack(ms)


def setup_inputs(seed: int = 0) -> dict:
    key = jax.random.key(seed)
    ks = jax.random.split(key, 24)
    f32 = jnp.float32
    nrm = lambda k, shape, s: jax.random.normal(k, shape, f32) * s
    a_c = jax.random.uniform(ks[0], (DEPTH, W_A), f32, 0.9, 0.999)
    s = a_c ** (1.0 / LRU_C)
    lru_lambda = jnp.log(s) - jnp.log1p(-s)
    b_mf = jnp.linspace(3.0, 6.0, H_B, dtype=f32)[None, :] + nrm(ks[1], (DEPTH, H_B), 0.1)
    return {
        "x_prompt": nrm(ks[2], (BATCH, SEQ, D_MODEL), 1.0),
        "x_sample": nrm(ks[3], (DEC_BATCH, DEC_SEQ, D_MODEL), 1.0),
        "state_rglru_h": nrm(ks[4], (DEPTH, DEC_BATCH, W_A), 0.5),
        "state_rglru_conv": nrm(ks[5], (DEPTH, DEC_BATCH, CONV_W - 1, W_A), 1.0),
        "state_mlstm_C": nrm(ks[6], (DEPTH, DEC_BATCH, H_B, DK, DV), 0.1),
        "state_mlstm_n": nrm(ks[7], (DEPTH, DEC_BATCH, H_B, DK), 0.5),
        "state_mlstm_m": nrm(ks[8], (DEPTH, DEC_BATCH, H_B), 0.5),
        "g_norm": 1.0 + nrm(ks[9], (DEPTH, D_MODEL), 0.02),
        "w_in": nrm(ks[10], (DEPTH, D_MODEL, D_IN), D_MODEL ** -0.5),
        "conv_w": nrm(ks[11], (DEPTH, CONV_W, W_A), CONV_W ** -0.5),
        "conv_b": nrm(ks[12], (DEPTH, W_A), 0.05),
        "w_rgate": nrm(ks[13], (DEPTH, NB_A, BW_A, BW_A), BW_A ** -0.5),
        "b_rgate": nrm(ks[14], (DEPTH, W_A), 0.1),
        "w_igate": nrm(ks[15], (DEPTH, NB_A, BW_A, BW_A), BW_A ** -0.5),
        "b_igate": nrm(ks[16], (DEPTH, W_A), 0.1),
        "lru_lambda": lru_lambda,
        "b_mi": nrm(ks[17], (DEPTH, H_B), 0.1),
        "b_mf": b_mf,
        "g_mhead": 1.0 + nrm(ks[18], (DEPTH, W_B), 0.02),
        "w_out": nrm(ks[19], (DEPTH, W_A + W_B, D_MODEL), (W_A + W_B) ** -0.5),
        "g_final": 1.0 + nrm(ks[20], (D_MODEL,), 0.02),
    }


def reference(x_prompt, x_sample, state_rglru_h, state_rglru_conv, state_mlstm_C, state_mlstm_n,
              state_mlstm_m, g_norm, w_in, conv_w, conv_b, w_rgate, b_rgate, w_igate, b_igate,
              lru_lambda, b_mi, b_mf, g_mhead, w_out, g_final):
    Bp = x_prompt.shape[0]
    zh = jnp.zeros((DEPTH, Bp, W_A), state_rglru_h.dtype)
    zconv = jnp.zeros((DEPTH, Bp, CONV_W - 1, W_A), state_rglru_conv.dtype)
    zC = jnp.zeros((DEPTH, Bp, H_B, DK, DV), state_mlstm_C.dtype)
    zn = jnp.zeros((DEPTH, Bp, H_B, DK), state_mlstm_n.dtype)
    zm = jnp.zeros((DEPTH, Bp, H_B), state_mlstm_m.dtype)
    y_prompt, p_h, p_conv, p_C, p_n, p_m = trunk(
        x_prompt, zh, zconv, zC, zn, zm, g_norm, w_in, conv_w, conv_b, w_rgate, b_rgate,
        w_igate, b_igate, lru_lambda, b_mi, b_mf, g_mhead, w_out, g_final)
    y_sample, s_h, s_conv, s_C, s_n, s_m = trunk(
        x_sample, state_rglru_h, state_rglru_conv, state_mlstm_C, state_mlstm_n, state_mlstm_m,
        g_norm, w_in, conv_w, conv_b, w_rgate, b_rgate, w_igate, b_igate, lru_lambda, b_mi, b_mf,
        g_mhead, w_out, g_final)
    return (y_prompt, y_sample, p_h, p_conv, p_C, p_n, p_m, s_h, s_conv, s_C, s_n, s_m)
```

```python
import functools

import jax
import jax.numpy as jnp
from jax import lax
from jax.experimental import pallas as pl
from jax.experimental.pallas import tpu as pltpu

f32 = jnp.float32
bf16 = jnp.bfloat16

D_MODEL = 1024
W_A = 1024
W_B = 1024
NB_A = 16
BW_A = W_A // NB_A
CONV_W = 4
LRU_C = 8.0
H_B = 4
DK = W_B // H_B
DV = W_B // H_B
CHUNK = 128
EPS = 1e-6
DEPTH = 2

MXU_TILE = 256
SUBLANES = 8
VMEM_LIMIT_BYTES = 56 * 1024 * 1024

GW = MXU_TILE
NG_A = W_A // GW
G_XA, G_ZA, G_Q, G_K, G_V, G_O, G_ZB, G_IF = 0, 4, 8, 12, 16, 20, 24, 28
N_GROUPS = 29

PROMPT_TILE = 512
DEC_BB = 4


def _mm(a, b):
    return jnp.dot(a, b, preferred_element_type=f32)


def _rms_scale(x, g):
    return x * lax.rsqrt(jnp.mean(x * x, axis=-1, keepdims=True) + EPS) * g


def _rglru_gates(xc, wr, wi, br, bi, sp):
    xcb = xc.astype(bf16)
    r = jax.nn.sigmoid(_mm(xcb, wr) + br)
    i = jax.nn.sigmoid(_mm(xcb, wi) + bi)
    log_a = (-LRU_C) * r * sp
    a = jnp.exp(log_a)
    mult = jnp.sqrt(-jnp.tanh(log_a) * (1.0 + a * a))
    return a, mult * (i * xc)


def _head_norm_gate(o, hb, zb, gm):
    yb = jax.nn.sigmoid(o) * hb
    yb = yb * lax.rsqrt(jnp.mean(yb * yb, axis=-1, keepdims=True) + EPS)
    return yb * gm * jax.nn.silu(zb)


def _prompt_layer_body(last_layer, tile,
                       x_ref, gn_ref, win_ref, wift_ref, bcol_ref, brow_ref,
                       cw_ref, cb_ref, wr_ref, wi_ref, br_ref, bi_ref, lam_ref,
                       gm_ref, wout_ref, gf_ref, tril_ref, triu_ref,
                       y_ref, h_ref, conv_ref, c_ref, n_ref, m_ref,
                       hn_s, xa_s, a_s, u_s, mrg_s, q_s, k_s, v_s, hb_s, gcol_s, grow_s):
    M = tile
    n_chunks = M // CHUNK
    t = pl.program_id(1)

    @pl.when(t == 0)
    def _init():
        h_ref[...] = jnp.zeros_like(h_ref)
        c_ref[...] = jnp.zeros_like(c_ref)
        n_ref[...] = jnp.zeros_like(n_ref)
        m_ref[...] = jnp.zeros_like(m_ref)
        for g in range(NG_A):
            xa_s[g, 0:SUBLANES, :] = jnp.zeros((SUBLANES, GW), f32)

    hn_s[...] = _rms_scale(x_ref[...], gn_ref[...]).astype(bf16)

    gcol_s[...] = _mm(hn_s[...], win_ref[G_IF]) + bcol_ref[...]
    grow = lax.dot_general(wift_ref[...], hn_s[...], (((1,), (1,)), ((), ())),
                           preferred_element_type=f32) + brow_ref[...]
    for j in range(n_chunks):
        grow_s[j] = grow[:, j * CHUNK:(j + 1) * CHUNK]

    sp_all = jax.nn.softplus(-lam_ref[...])
    for g in range(NG_A):
        cs = slice(g * GW, (g + 1) * GW)
        xa_s[g, SUBLANES:SUBLANES + M, :] = _mm(hn_s[...], win_ref[G_XA + g])
        xc = cb_ref[:, cs] + cw_ref[3:4, cs] * xa_s[g, SUBLANES:SUBLANES + M, :]
        for j in range(CONV_W - 1):
            off = SUBLANES - (CONV_W - 1) + j
            xc = xc + cw_ref[j:j + 1, cs] * xa_s[g, off:off + M, :]
        a, u = _rglru_gates(xc, wr_ref[g], wi_ref[g], br_ref[:, cs], bi_ref[:, cs], sp_all[:, cs])
        a_s[...] = a
        u_s[...] = u

        def scan_block(j, hprev):
            r0 = pl.multiple_of(j * SUBLANES, SUBLANES)
            A = a_s[pl.ds(r0, SUBLANES), :]
            B = u_s[pl.ds(r0, SUBLANES), :]
            row = lax.broadcasted_iota(jnp.int32, (SUBLANES, GW), 0)
            for sh in (1, 2, 4):
                keep = row >= sh
                A_sh = pltpu.roll(A, sh, axis=0)
                B_sh = pltpu.roll(B, sh, axis=0)
                B = jnp.where(keep, A * B_sh + B, B)
                A = jnp.where(keep, A * A_sh, A)
            hblk = A * hprev + B
            u_s[pl.ds(r0, SUBLANES), :] = hblk
            return hblk[SUBLANES - 1:SUBLANES, :]

        hlast = lax.fori_loop(0, M // SUBLANES, scan_block, h_ref[:, cs], unroll=4)
        h_ref[:, cs] = hlast
        xa_s[g, SUBLANES - (CONV_W - 1):SUBLANES, :] = xa_s[g, SUBLANES + M - (CONV_W - 1):SUBLANES + M, :]
        za = _mm(hn_s[...], win_ref[G_ZA + g])
        mrg_s[:, cs] = (u_s[...] * jax.nn.silu(za)).astype(bf16)

    @pl.when(t == pl.num_programs(1) - 1)
    def _conv_out():
        for g in range(NG_A):
            conv_ref[:, g * GW:(g + 1) * GW] = xa_s[g, SUBLANES - (CONV_W - 1):SUBLANES, :]

    for h in range(H_B):
        q_s[h] = _mm(hn_s[...], win_ref[G_Q + h]).astype(bf16)
        k_s[h] = _mm(hn_s[...], win_ref[G_K + h]) * (DK ** -0.5)
        v_s[h] = _mm(hn_s[...], win_ref[G_V + h]).astype(bf16)

    tril = tril_ref[...]
    causal = tril > 0.0

    def chunk_body(ci, m_carry):
        r0 = pl.multiple_of(ci * CHUNK, CHUNK)
        rows = pl.ds(r0, CHUNK)
        gc = gcol_s[rows, 0:CHUNK]
        lf_c = jax.nn.log_sigmoid(gc)
        b_c = jnp.dot(tril, lf_c, preferred_element_type=f32, precision=lax.Precision.HIGHEST)
        gr = grow_s[ci]
        lf_r = jax.nn.log_sigmoid(gr)
        b_r = jnp.dot(lf_r, triu_ref[...], preferred_element_type=f32, precision=lax.Precision.HIGHEST)
        m_new = []
        for h in range(H_B):
            m_prev = m_carry[h]
            bcol = b_c[:, H_B + h:H_B + h + 1]
            igcol = gc[:, h:h + 1]
            brow = b_r[H_B + h:H_B + h + 1, :]
            igrow = gr[h:h + 1, :]
            dmat = jnp.where(causal, bcol - brow + igrow, -jnp.inf)
            gcarry = bcol + m_prev
            m_t = jnp.maximum(gcarry, jnp.max(dmat, axis=-1, keepdims=True))
            w = jnp.exp(dmat - m_t)
            ginter = jnp.exp(gcarry - m_t)
            qh = q_s[h, rows, :]
            kh = k_s[h, rows, :]
            vh = v_s[h, rows, :]
            s = lax.dot_general(qh, kh.astype(bf16), (((1,), (1,)), ((), ())), preferred_element_type=f32)
            p = w * s
            ch = c_ref[h]
            nh = n_ref[h:h + 1, :]
            num = _mm(p.astype(bf16), vh) + ginter * _mm(qh, ch.astype(bf16))
            den = (jnp.sum(p, axis=-1, keepdims=True)
                   + ginter * jnp.sum(qh.astype(f32) * nh, axis=-1, keepdims=True))
            denom = jnp.maximum(jnp.abs(den), jnp.exp(-m_t))
            hb_s[h, rows, :] = num * (1.0 / denom)
            m_last = m_t[CHUNK - 1:CHUNK, :]
            b_last = bcol[CHUNK - 1:CHUNK, :]
            decay = jnp.exp(b_last - bcol + igcol - m_last)
            cscale = jnp.exp(b_last + m_prev - m_last)
            dk = decay * kh
            c_ref[h] = cscale * ch + lax.dot_general(dk.astype(bf16), vh, (((0,), (0,)), ((), ())),
                                                     preferred_element_type=f32)
            n_ref[h:h + 1, :] = cscale * nh + jnp.sum(dk, axis=0, keepdims=True)
            m_new.append(m_last)
        return tuple(m_new)

    m0 = tuple(m_ref[:, h:h + 1] for h in range(H_B))
    m_fin = lax.fori_loop(0, n_chunks, chunk_body, m0)
    for h in range(H_B):
        m_ref[:, h:h + 1] = m_fin[h]

    for h in range(H_B):
        cs = slice(h * DV, (h + 1) * DV)
        o = _mm(hn_s[...], win_ref[G_O + h])
        zb = _mm(hn_s[...], win_ref[G_ZB + h])
        mrg_s[:, W_A + h * DV:W_A + (h + 1) * DV] = _head_norm_gate(o, hb_s[h], zb, gm_ref[:, cs]).astype(bf16)

    out = x_ref[...] + _mm(mrg_s[...], wout_ref[...])
    if last_layer:
        out = _rms_scale(out, gf_ref[...])
    y_ref[...] = out


def _const_spec(shape):
    nd = len(shape)
    return pl.BlockSpec(shape, lambda b, t: (0,) * nd, pipeline_mode=pl.Buffered(1))


def _prompt_layer(x, lw, g_final, last_layer):
    B, S, _ = x.shape
    M = PROMPT_TILE
    n_chunks = M // CHUNK
    body = functools.partial(_prompt_layer_body, last_layer, M)
    out_shape = (
        jax.ShapeDtypeStruct((B, S, D_MODEL), f32),
        jax.ShapeDtypeStruct((B, 1, W_A), f32),
        jax.ShapeDtypeStruct((B, CONV_W - 1, W_A), f32),
        jax.ShapeDtypeStruct((B, H_B, DK, DV), f32),
        jax.ShapeDtypeStruct((B, H_B, DK), f32),
        jax.ShapeDtypeStruct((B, 1, 128), f32),
    )
    in_specs = [
        pl.BlockSpec((None, M, D_MODEL), lambda b, t: (b, t, 0)),
        _const_spec((1, D_MODEL)),
        _const_spec((N_GROUPS, D_MODEL, GW)),
        _const_spec((SUBLANES, D_MODEL)),
        _const_spec((1, GW)),
        _const_spec((SUBLANES, 1)),
        _const_spec((CONV_W, W_A)),
        _const_spec((1, W_A)),
        _const_spec((NG_A, GW, GW)),
        _const_spec((NG_A, GW, GW)),
        _const_spec((1, W_A)),
        _const_spec((1, W_A)),
        _const_spec((1, W_A)),
        _const_spec((1, W_B)),
        _const_spec((W_A + W_B, D_MODEL)),
        _const_spec((1, D_MODEL)),
        _const_spec((CHUNK, CHUNK)),
        _const_spec((CHUNK, CHUNK)),
    ]
    out_specs = (
        pl.BlockSpec((None, M, D_MODEL), lambda b, t: (b, t, 0)),
        pl.BlockSpec((None, 1, W_A), lambda b, t: (b, 0, 0)),
        pl.BlockSpec((None, CONV_W - 1, W_A), lambda b, t: (b, 0, 0)),
        pl.BlockSpec((None, H_B, DK, DV), lambda b, t: (b, 0, 0, 0)),
        pl.BlockSpec((None, H_B, DK), lambda b, t: (b, 0, 0)),
        pl.BlockSpec((None, 1, 128), lambda b, t: (b, 0, 0)),
    )
    scratch = [
        pltpu.VMEM((M, D_MODEL), bf16),
        pltpu.VMEM((NG_A, M + SUBLANES, GW), f32),
        pltpu.VMEM((M, GW), f32),
        pltpu.VMEM((M, GW), f32),
        pltpu.VMEM((M, W_A + W_B), bf16),
        pltpu.VMEM((H_B, M, DK), bf16),
        pltpu.VMEM((H_B, M, DK), f32),
        pltpu.VMEM((H_B, M, DV), bf16),
        pltpu.VMEM((H_B, M, DV), f32),
        pltpu.VMEM((M, GW), f32),
        pltpu.VMEM((n_chunks, SUBLANES, CHUNK), f32),
    ]
    return pl.pallas_call(
        body,
        out_shape=out_shape,
        grid=(B, S // M),
        in_specs=in_specs,
        out_specs=out_specs,
        scratch_shapes=scratch,
        compiler_params=pltpu.CompilerParams(
            dimension_semantics=("arbitrary", "arbitrary"),
            vmem_limit_bytes=VMEM_LIMIT_BYTES),
        name="prompt_layer_last" if last_layer else "prompt_layer",
    )(x, lw["g_norm"], lw["win"], lw["wift"], lw["bcol"], lw["brow"], lw["conv_w"], lw["conv_b"],
      lw["wr"], lw["wi"], lw["b_r"], lw["b_i"], lw["lam"], lw["g_mhead"], lw["wout"], g_final,
      lw["tril"], lw["triu"])


def _dec_inproj_body(x_ref, gn_ref, win_ref, u_ref, hn_s):
    @pl.when(pl.program_id(0) == 0)
    def _():
        hn_s[...] = _rms_scale(x_ref[...], gn_ref[...]).astype(bf16)
    u_ref[...] = _mm(hn_s[...], win_ref[...])


def _dec_inproj(x, lw):
    nb = x.shape[0]
    return pl.pallas_call(
        _dec_inproj_body,
        out_shape=jax.ShapeDtypeStruct((nb, N_GROUPS * GW), f32),
        grid=(N_GROUPS,),
        in_specs=[
            pl.BlockSpec((nb, D_MODEL), lambda j: (0, 0)),
            pl.BlockSpec((1, D_MODEL), lambda j: (0, 0)),
            pl.BlockSpec((None, D_MODEL, GW), lambda j: (j, 0, 0)),
        ],
        out_specs=pl.BlockSpec((nb, GW), lambda j: (0, j)),
        scratch_shapes=[pltpu.VMEM((nb, D_MODEL), bf16)],
        compiler_params=pltpu.CompilerParams(dimension_semantics=("arbitrary",)),
        name="dec_inproj",
    )(x, lw["g_norm"], lw["win"])


def _dec_rglru_body(u_ref, h0_ref, conv0_ref, cw_ref, cb_ref, wr_ref, wi_ref, br_ref, bi_ref, lam_ref,
                    h_ref, conv_ref, mrg_ref):
    sp_all = jax.nn.softplus(-lam_ref[...])
    for g in range(NG_A):
        cs = slice(g * GW, (g + 1) * GW)
        xa = u_ref[:, G_XA * GW + g * GW:G_XA * GW + (g + 1) * GW]
        za = u_ref[:, G_ZA * GW + g * GW:G_ZA * GW + (g + 1) * GW]
        xc = cb_ref[:, cs] + cw_ref[CONV_W - 1:CONV_W, cs] * xa
        for j in range(CONV_W - 1):
            xc = xc + cw_ref[j:j + 1, cs] * conv0_ref[:, j * W_A + g * GW:j * W_A + (g + 1) * GW]
        a, u = _rglru_gates(xc, wr_ref[g], wi_ref[g], br_ref[:, cs], bi_ref[:, cs], sp_all[:, cs])
        h = a * h0_ref[:, cs] + u
        h_ref[:, cs] = h
        mrg_ref[:, cs] = h * jax.nn.silu(za)
        for j in range(CONV_W - 2):
            conv_ref[:, j * W_A + g * GW:j * W_A + (g + 1) * GW] = (
                conv0_ref[:, (j + 1) * W_A + g * GW:(j + 1) * W_A + (g + 1) * GW])
        conv_ref[:, (CONV_W - 2) * W_A + g * GW:(CONV_W - 2) * W_A + (g + 1) * GW] = xa


def _dec_rglru(u, h0_all, conv0_all, layer, lw):
    nb = u.shape[0]
    full = lambda shape: pl.BlockSpec(shape, lambda i: (0,) * len(shape))
    return pl.pallas_call(
        _dec_rglru_body,
        out_shape=(
            jax.ShapeDtypeStruct((nb, W_A), f32),
            jax.ShapeDtypeStruct((nb, (CONV_W - 1) * W_A), f32),
            jax.ShapeDtypeStruct((nb, W_A), f32),
        ),
        grid=(1,),
        in_specs=[
            pl.BlockSpec((nb, 2 * W_A), lambda i: (0, 0)),
            pl.BlockSpec((None, nb, W_A), lambda i: (layer, 0, 0)),
            pl.BlockSpec((None, nb, (CONV_W - 1) * W_A), lambda i: (layer, 0, 0)),
            full((CONV_W, W_A)), full((1, W_A)),
            full((NG_A, GW, GW)), full((NG_A, GW, GW)),
            full((1, W_A)), full((1, W_A)), full((1, W_A)),
        ],
        out_specs=(full((nb, W_A)), full((nb, (CONV_W - 1) * W_A)), full((nb, W_A))),
        compiler_params=pltpu.CompilerParams(dimension_semantics=("arbitrary",)),
        name="dec_rglru",
    )(u, h0_all, conv0_all, lw["conv_w"], lw["conv_b"], lw["wr"], lw["wi"], lw["b_r"], lw["b_i"], lw["lam"])


def _dec_mlstm_body(u_ref, c0_ref, n0_ref, m0_ref, bias_ref, gm_ref, c_ref, n_ref, m_ref, mrg_ref):
    def one(bi, carry):
        urow = u_ref[bi]
        gates = urow[:, 5 * W_B:5 * W_B + 2 * H_B] + bias_ref[...]
        qk = jnp.concatenate([urow[:, j * DK:(j + 1) * DK] for j in range(2 * H_B)], axis=0)
        cols = qk.T
        m_prev_all = m0_ref[bi]
        for h in range(H_B):
            ig = gates[:, h:h + 1]
            lf = jax.nn.log_sigmoid(gates[:, H_B + h:H_B + h + 1])
            m_prev = m_prev_all[:, h:h + 1]
            gcarry = lf + m_prev
            m_t = jnp.maximum(gcarry, ig)
            w = jnp.exp(ig - m_t)
            ginter = jnp.exp(gcarry - m_t)
            q = urow[:, h * DK:(h + 1) * DK]
            k = urow[:, W_B + h * DK:W_B + (h + 1) * DK] * (DK ** -0.5)
            v = urow[:, 2 * W_B + h * DV:2 * W_B + (h + 1) * DV]
            o = urow[:, 3 * W_B + h * DV:3 * W_B + (h + 1) * DV]
            zb = urow[:, 4 * W_B + h * DV:4 * W_B + (h + 1) * DV]
            qcol = cols[:, h:h + 1]
            kcol = cols[:, H_B + h:H_B + h + 1] * (DK ** -0.5)
            ch = c0_ref[bi, h]
            nh = n0_ref[bi, h:h + 1, :]
            qc = jnp.sum(ch * qcol, axis=0, keepdims=True)
            p = w * jnp.sum(q * k, axis=-1, keepdims=True)
            num = p * v + ginter * qc
            den = p + ginter * jnp.sum(q * nh, axis=-1, keepdims=True)
            denom = jnp.maximum(jnp.abs(den), jnp.exp(-m_t))
            hb = num / denom
            c_ref[bi, h] = ginter * ch + kcol * (w * v)
            n_ref[bi, h:h + 1, :] = ginter * nh + w * k
            m_ref[bi, :, h:h + 1] = m_t
            mrg_ref[bi, :, h * DV:(h + 1) * DV] = _head_norm_gate(o, hb, zb, gm_ref[:, h * DV:(h + 1) * DV])
        return carry

    lax.fori_loop(0, DEC_BB, one, 0)


def _dec_mlstm(u3, c0_all, n0_all, m0_all, layer, lw):
    nb = u3.shape[0]
    ucols = 5 * W_B + GW
    return pl.pallas_call(
        _dec_mlstm_body,
        out_shape=(
            jax.ShapeDtypeStruct((nb, H_B, DK, DV), f32),
            jax.ShapeDtypeStruct((nb, H_B, DK), f32),
            jax.ShapeDtypeStruct((nb, 1, H_B), f32),
            jax.ShapeDtypeStruct((nb, 1, W_B), f32),
        ),
        grid=(nb // DEC_BB,),
        in_specs=[
            pl.BlockSpec((DEC_BB, 1, ucols), lambda i: (i, 0, 0)),
            pl.BlockSpec((None, DEC_BB, H_B, DK, DV), lambda i: (layer, i, 0, 0, 0)),
            pl.BlockSpec((None, DEC_BB, H_B, DK), lambda i: (layer, i, 0, 0)),
            pl.BlockSpec((None, DEC_BB, 1, H_B), lambda i: (layer, i, 0, 0)),
            pl.BlockSpec((1, 2 * H_B), lambda i: (0, 0)),
            pl.BlockSpec((1, W_B), lambda i: (0, 0)),
        ],
        out_specs=(
            pl.BlockSpec((DEC_BB, H_B, DK, DV), lambda i: (i, 0, 0, 0)),
            pl.BlockSpec((DEC_BB, H_B, DK), lambda i: (i, 0, 0)),
            pl.BlockSpec((DEC_BB, 1, H_B), lambda i: (i, 0, 0)),
            pl.BlockSpec((DEC_BB, 1, W_B), lambda i: (i, 0, 0)),
        ),
        compiler_params=pltpu.CompilerParams(dimension_semantics=("arbitrary",),
                                             vmem_limit_bytes=VMEM_LIMIT_BYTES),
        name="dec_mlstm",
    )(u3, c0_all, n0_all, m0_all, lw["bias8"], lw["g_mhead"])


def _dec_out_body(last_layer, x_ref, ma_ref, mb_ref, wout_ref, gf_ref, y_ref):
    out = (x_ref[...] + _mm(ma_ref[...].astype(bf16), wout_ref[0:W_A, :])
           + _mm(mb_ref[...].astype(bf16), wout_ref[W_A:W_A + W_B, :]))
    if last_layer:
        out = _rms_scale(out, gf_ref[...])
    y_ref[...] = out


def _dec_out(x, ma, mb, lw, g_final, last_layer):
    nb = x.shape[0]
    full = lambda shape: pl.BlockSpec(shape, lambda i: (0,) * len(shape))
    return pl.pallas_call(
        functools.partial(_dec_out_body, last_layer),
        out_shape=jax.ShapeDtypeStruct((nb, D_MODEL), f32),
        grid=(1,),
        in_specs=[full((nb, D_MODEL)), full((nb, W_A)), full((nb, W_B)),
                  full((W_A + W_B, D_MODEL)), full((1, D_MODEL))],
        out_specs=full((nb, D_MODEL)),
        compiler_params=pltpu.CompilerParams(dimension_semantics=("arbitrary",)),
        name="dec_out_last" if last_layer else "dec_out",
    )(x, ma, mb, lw["wout"], g_final)


def _block_diag_groups(w):
    per = GW // BW_A
    wg = w.reshape(NG_A, per, BW_A, BW_A)
    eye = jnp.eye(per, dtype=w.dtype)
    dense = jnp.einsum("gpcd,pq->gpcqd", wg, eye).reshape(NG_A, GW, GW)
    return dense.astype(bf16)


def _layer_weights(l, g_norm, w_in, conv_w, conv_b, w_rgate, b_rgate, w_igate, b_igate,
                   lru_lambda, b_mi, b_mf, g_mhead, w_out):
    n_full = N_GROUPS - 1
    wfull = w_in[l][:, :n_full * GW].reshape(D_MODEL, n_full, GW).transpose(1, 0, 2)
    wif = w_in[l][:, n_full * GW:]
    wif_pad = jnp.pad(wif, ((0, 0), (0, GW - 2 * H_B)))[None]
    bias8 = jnp.concatenate([b_mi[l], b_mf[l]])[None, :]
    tril = jnp.tril(jnp.ones((CHUNK, CHUNK), f32))
    return {
        "g_norm": g_norm[l][None, :],
        "win": jnp.concatenate([wfull, wif_pad], axis=0).astype(bf16),
        "wift": wif.T.astype(bf16),
        "bias8": bias8,
        "bcol": jnp.pad(bias8, ((0, 0), (0, GW - 2 * H_B))),
        "brow": bias8.T,
        "conv_w": conv_w[l], "conv_b": conv_b[l][None, :],
        "wr": _block_diag_groups(w_rgate[l]), "wi": _block_diag_groups(w_igate[l]),
        "b_r": b_rgate[l][None, :], "b_i": b_igate[l][None, :], "lam": lru_lambda[l][None, :],
        "g_mhead": g_mhead[l][None, :],
        "wout": w_out[l].astype(bf16),
        "tril": tril, "triu": tril.T,
    }


def kernel(x_prompt, x_sample, state_rglru_h, state_rglru_conv, state_mlstm_C, state_mlstm_n, state_mlstm_m, g_norm, w_in, conv_w, conv_b, w_rgate, b_rgate, w_igate, b_igate, lru_lambda, b_mi, b_mf, g_mhead, w_out, g_final):
    lws = [_layer_weights(l, g_norm, w_in, conv_w, conv_b, w_rgate, b_rgate, w_igate, b_igate,
                          lru_lambda, b_mi, b_mf, g_mhead, w_out) for l in range(DEPTH)]
    gf = g_final[None, :]

    x = x_prompt
    p_h, p_conv, p_c, p_n, p_m = [], [], [], [], []
    for l in range(DEPTH):
        x, hl, cv, cc, nn, mm = _prompt_layer(x, lws[l], gf, l == DEPTH - 1)
        p_h.append(hl[:, 0, :]); p_conv.append(cv); p_c.append(cc); p_n.append(nn)
        p_m.append(mm[:, 0, :H_B])
    y_prompt = x

    nb = x_sample.shape[0]
    xs = x_sample.reshape(nb, D_MODEL)
    conv0 = state_rglru_conv.reshape(DEPTH, nb, (CONV_W - 1) * W_A)
    m0 = state_mlstm_m.reshape(DEPTH, nb, 1, H_B)
    s_h, s_conv, s_c, s_n, s_m = [], [], [], [], []
    for l in range(DEPTH):
        u = _dec_inproj(xs, lws[l])
        hl, cv, ma = _dec_rglru(u, state_rglru_h, conv0, l, lws[l])
        u3 = u[:, G_Q * GW:].reshape(nb, 1, 5 * W_B + GW)
        cc, nn, mm, mb = _dec_mlstm(u3, state_mlstm_C, state_mlstm_n, m0, l, lws[l])
        xs = _dec_out(xs, ma, mb.reshape(nb, W_B), lws[l], gf, l == DEPTH - 1)
        s_h.append(hl); s_conv.append(cv.reshape(nb, CONV_W - 1, W_A)); s_c.append(cc); s_n.append(nn)
        s_m.append(mm.reshape(nb, H_B))
    y_sample = xs.reshape(nb, 1, D_MODEL)

    st = jnp.stack
    return (y_prompt, y_sample, st(p_h), st(p_conv), st(p_c), st(p_n), st(p_m),
            st(s_h), st(s_conv), st(s_c), st(s_n), st(s_m))
```

```python
import functools

import jax
import jax.numpy as jnp
from jax import lax
from jax.experimental import pallas as pl
from jax.experimental.pallas import tpu as pltpu

f32 = jnp.float32
bf16 = jnp.bfloat16

D_MODEL = 1024
W_A = 1024
W_B = 1024
NB_A = 16
BW_A = W_A // NB_A
CONV_W = 4
LRU_C = 8.0
H_B = 4
DK = W_B // H_B
DV = W_B // H_B
CHUNK = 128
EPS = 1e-6
DEPTH = 2

MXU_TILE = 256
LANES = 128
SUBLANES = 8
VMEM_LIMIT_BYTES = 56 * 1024 * 1024

GW = MXU_TILE
NG_A = W_A // GW
C_XA, C_ZA = 0, W_A
C_Q = 2 * W_A
C_K, C_V, C_O, C_ZB = C_Q + W_B, C_Q + 2 * W_B, C_Q + 3 * W_B, C_Q + 4 * W_B
C_IF = C_Q + 5 * W_B
D_IN_PAD = C_IF + GW

PROMPT_TILE = 512
SEG = PROMPT_TILE // SUBLANES
DEC_BB = 4


def _mm(a, b):
    return jnp.dot(a, b, preferred_element_type=f32)


def _rms_scale(x, g):
    return x * lax.rsqrt(jnp.mean(x * x, axis=-1, keepdims=True) + EPS) * g


def _rglru_gates(xc, wr, wi, br, bi, sp):
    xcb = xc.astype(bf16)
    r = jax.nn.sigmoid(_mm(xcb, wr) + br)
    i = jax.nn.sigmoid(_mm(xcb, wi) + bi)
    log_a = (-LRU_C) * r * sp
    a = jnp.exp(log_a)
    mult = jnp.sqrt(-jnp.tanh(log_a) * (1.0 + a * a))
    return a, mult * (i * xc)


def _head_norm_gate(o, hb, zb, gm):
    yb = jax.nn.sigmoid(o) * hb
    yb = yb * lax.rsqrt(jnp.mean(yb * yb, axis=-1, keepdims=True) + EPS)
    return yb * gm * jax.nn.silu(zb)


def _twice(x):
    return jnp.concatenate([x, x], axis=1)


def _skip_alias_refs(body, n_in, n_alias):
    def wrapped(*refs):
        return body(*refs[:n_in], *refs[n_in + n_alias:])
    return wrapped


def _prompt_layer_body(last_layer,
                       x_ref, gn_ref, win_ref, wift_ref, wkt_ref, bcol_ref, brow_ref,
                       cw_ref, cb_ref, wr_ref, wi_ref, br_ref, bi_ref, lam_ref,
                       gm_ref, wout_ref, gf_ref, tril_ref, triu_ref,
                       y_ref, h_ref, conv_ref, c_ref, n_ref, m_ref,
                       hn_s, xa_s, a_s, u_s, mrg_s, q_s, kt_s, v_s, hb_s, gcol_s, grow_s, nrep_s):
    M = PROMPT_TILE
    n_chunks = M // CHUNK
    n_lt = W_A // LANES
    t = pl.program_id(1)
    hist = CONV_W - 1

    @pl.when(t == 0)
    def _init():
        h_ref[...] = jnp.zeros_like(h_ref)
        c_ref[...] = jnp.zeros_like(c_ref)
        m_ref[...] = jnp.zeros_like(m_ref)
        nrep_s[...] = jnp.zeros_like(nrep_s)
        xa_s[0:SUBLANES, :] = jnp.zeros((SUBLANES, W_A), f32)

    hn_s[...] = _rms_scale(x_ref[...], gn_ref[...]).astype(bf16)

    gcol_s[...] = _mm(hn_s[...], win_ref[:, C_IF:C_IF + GW])[:, 0:LANES] + bcol_ref[...]
    grow = lax.dot_general(wift_ref[...], hn_s[...], (((1,), (1,)), ((), ())),
                           preferred_element_type=f32) + brow_ref[...]
    for j in range(n_chunks):
        grow_s[j] = grow[:, j * CHUNK:(j + 1) * CHUNK]

    sp_all = jax.nn.softplus(-lam_ref[...])
    for g in range(NG_A):
        cs = slice(g * GW, (g + 1) * GW)
        xa_s[SUBLANES:SUBLANES + M, cs] = _mm(hn_s[...], win_ref[:, C_XA + g * GW:C_XA + (g + 1) * GW])
        xc = cb_ref[:, cs] + cw_ref[hist:hist + 1, cs] * xa_s[SUBLANES:SUBLANES + M, cs]
        for j in range(hist):
            off = SUBLANES - hist + j
            xc = xc + cw_ref[j:j + 1, cs] * xa_s[off:off + M, cs]
        a, u = _rglru_gates(xc, wr_ref[g], wi_ref[g], br_ref[:, cs], bi_ref[:, cs], sp_all[:, cs])
        for half in range(GW // LANES):
            lt = g * (GW // LANES) + half
            a_s[lt] = a[:, half * LANES:(half + 1) * LANES]
            u_s[lt] = u[:, half * LANES:(half + 1) * LANES]

    @pl.when(t == pl.num_programs(1) - 1)
    def _conv_out():
        conv_ref[...] = xa_s[SUBLANES + M - hist:SUBLANES + M, :]

    xa_s[SUBLANES - hist:SUBLANES, :] = xa_s[SUBLANES + M - hist:SUBLANES + M, :]

    def seg_rows(j):
        return pl.ds(j, SUBLANES, stride=SEG)

    def pass1(j, carry):
        hs, ps = carry
        nh, npd = [], []
        for lt in range(n_lt):
            A = a_s[lt, seg_rows(j), :]
            nh.append(A * hs[lt] + u_s[lt, seg_rows(j), :])
            npd.append(A * ps[lt])
        return tuple(nh), tuple(npd)

    zeros = tuple(jnp.zeros((SUBLANES, LANES), f32) for _ in range(n_lt))
    ones = tuple(jnp.ones((SUBLANES, LANES), f32) for _ in range(n_lt))
    hend, pend = lax.fori_loop(0, SEG, pass1, (zeros, ones), unroll=2)

    starts = []
    for lt in range(n_lt):
        c = h_ref[:, lt * LANES:(lt + 1) * LANES]
        rows = []
        for s in range(SUBLANES):
            rows.append(c)
            c = pend[lt][s:s + 1, :] * c + hend[lt][s:s + 1, :]
        h_ref[:, lt * LANES:(lt + 1) * LANES] = c
        starts.append(jnp.concatenate(rows, axis=0))

    def pass2(j, hs):
        nh = []
        for lt in range(n_lt):
            hnew = a_s[lt, seg_rows(j), :] * hs[lt] + u_s[lt, seg_rows(j), :]
            u_s[lt, seg_rows(j), :] = hnew
            nh.append(hnew)
        return tuple(nh)

    lax.fori_loop(0, SEG, pass2, tuple(starts), unroll=2)

    for g in range(NG_A):
        cs = slice(g * GW, (g + 1) * GW)
        za = _mm(hn_s[...], win_ref[:, C_ZA + g * GW:C_ZA + (g + 1) * GW])
        hg = jnp.concatenate([u_s[g * (GW // LANES) + half] for half in range(GW // LANES)], axis=1)
        mrg_s[:, cs] = (hg * jax.nn.silu(za)).astype(bf16)

    for h in range(H_B):
        q_s[h] = _mm(hn_s[...], win_ref[:, C_Q + h * DK:C_Q + (h + 1) * DK]).astype(bf16)
        v_s[h] = _mm(hn_s[...], win_ref[:, C_V + h * DV:C_V + (h + 1) * DV]).astype(bf16)
        kt = lax.dot_general(wkt_ref[h], hn_s[...], (((1,), (1,)), ((), ())),
                             preferred_element_type=f32) * (DK ** -0.5)
        for j in range(n_chunks):
            kt_s[h, j] = kt[:, j * CHUNK:(j + 1) * CHUNK]

    tril = tril_ref[...]
    causal = tril > 0.0

    def chunk_body(ci, m_carry):
        r0 = pl.multiple_of(ci * CHUNK, CHUNK)
        rows = pl.ds(r0, CHUNK)
        gc = gcol_s[rows, :]
        lf_c = jax.nn.log_sigmoid(gc)
        b_c = jnp.dot(tril, lf_c, preferred_element_type=f32, precision=lax.Precision.HIGHEST)
        gr = grow_s[ci]
        lf_r = jax.nn.log_sigmoid(gr)
        b_r = jnp.dot(lf_r, triu_ref[...], preferred_element_type=f32, precision=lax.Precision.HIGHEST)
        m_new = []
        for h in range(H_B):
            m_prev = m_carry[h]
            bcol = jnp.broadcast_to(b_c[:, H_B + h:H_B + h + 1], (CHUNK, LANES))
            brow = b_r[H_B + h:H_B + h + 1, :]
            igrow = gr[h:h + 1, :]
            dmat = jnp.where(causal, bcol - brow + igrow, -jnp.inf)
            gcarry = bcol + m_prev
            m_t = jnp.maximum(gcarry, jnp.max(dmat, axis=-1, keepdims=True))
            w = jnp.exp(dmat - m_t)
            ginter = jnp.exp(gcarry - m_t)
            qh = q_s[h, rows, :]
            vh = v_s[h, rows, :]
            kth = kt_s[h, ci]
            p = w * _mm(qh, kth.astype(bf16))
            ch = c_ref[h]
            nrep = nrep_s[h]
            num = _mm(p.astype(bf16), vh) + _twice(ginter) * _mm(qh, ch.astype(bf16))
            den = jnp.sum(p, axis=-1, keepdims=True) + ginter * _mm(qh, nrep.astype(bf16))
            denom = jnp.maximum(jnp.abs(den), jnp.exp(-m_t))
            hb_s[h, rows, :] = num * _twice(1.0 / denom)
            m_last = m_t[CHUNK - 1:CHUNK, :]
            b_last = bcol[CHUNK - 1:CHUNK, :]
            decay = jnp.exp(b_last - brow + igrow - m_last)
            cscale = jnp.exp(b_last + m_prev - m_last)
            dkt = kth * decay
            c_ref[h] = _twice(cscale) * ch + _mm(dkt.astype(bf16), vh)
            nrep_s[h] = cscale * nrep + jnp.sum(dkt, axis=-1, keepdims=True)
            m_new.append(m_last)
        return tuple(m_new)

    m0 = tuple(jnp.broadcast_to(m_ref[:, h:h + 1], (1, LANES)) for h in range(H_B))
    m_fin = lax.fori_loop(0, n_chunks, chunk_body, m0)
    for h in range(H_B):
        m_ref[:, h:h + 1] = m_fin[h][:, 0:1]

    @pl.when(t == pl.num_programs(1) - 1)
    def _n_out():
        for h in range(H_B):
            n_ref[h:h + 1, :] = nrep_s[h].T[0:1, :]

    for h in range(H_B):
        cs = slice(h * DV, (h + 1) * DV)
        o = _mm(hn_s[...], win_ref[:, C_O + h * DV:C_O + (h + 1) * DV])
        zb = _mm(hn_s[...], win_ref[:, C_ZB + h * DV:C_ZB + (h + 1) * DV])
        mrg_s[:, W_A + h * DV:W_A + (h + 1) * DV] = _head_norm_gate(o, hb_s[h], zb, gm_ref[:, cs]).astype(bf16)

    out = x_ref[...] + _mm(mrg_s[...], wout_ref[...])
    if last_layer:
        out = _rms_scale(out, gf_ref[...])
    y_ref[...] = out


def _prompt_layer(x, W, layer, prev_states):
    B, S, _ = x.shape
    M = PROMPT_TILE
    n_chunks = M // CHUNK
    last_layer = layer == DEPTH - 1

    def wspec(shape):
        nd = len(shape)
        return pl.BlockSpec((None,) + shape, lambda b, t: (layer,) + (0,) * nd, pipeline_mode=pl.Buffered(1))

    def cspec(shape):
        nd = len(shape)
        return pl.BlockSpec(shape, lambda b, t: (0,) * nd, pipeline_mode=pl.Buffered(1))

    in_specs = [
        pl.BlockSpec((None, M, D_MODEL), lambda b, t: (b, t, 0)),
        wspec((1, D_MODEL)),
        wspec((D_MODEL, D_IN_PAD)),
        wspec((SUBLANES, D_MODEL)),
        wspec((H_B, DK, D_MODEL)),
        wspec((1, LANES)),
        wspec((SUBLANES, 1)),
        wspec((CONV_W, W_A)),
        wspec((1, W_A)),
        wspec((NG_A, GW, GW)),
        wspec((NG_A, GW, GW)),
        wspec((1, W_A)),
        wspec((1, W_A)),
        wspec((1, W_A)),
        wspec((1, W_B)),
        wspec((W_A + W_B, D_MODEL)),
        cspec((1, D_MODEL)),
        cspec((CHUNK, CHUNK)),
        cspec((CHUNK, CHUNK)),
    ]
    args = [x, W["g_norm"], W["win"], W["wift"], W["wkt"], W["bcol"], W["brow"], W["conv_w"], W["conv_b"],
            W["wr"], W["wi"], W["b_r"], W["b_i"], W["lam"], W["g_mhead"], W["wout"], W["g_final"],
            W["tril"], W["triu"]]
    n_in = len(args)
    state_shapes = (
        jax.ShapeDtypeStruct((DEPTH, B, 1, W_A), f32),
        jax.ShapeDtypeStruct((DEPTH, B, CONV_W - 1, W_A), f32),
        jax.ShapeDtypeStruct((DEPTH, B, H_B, DK, DV), f32),
        jax.ShapeDtypeStruct((DEPTH, B, H_B, DK), f32),
        jax.ShapeDtypeStruct((DEPTH, B, 1, LANES), f32),
    )
    out_specs = (
        pl.BlockSpec((None, M, D_MODEL), lambda b, t: (b, t, 0)),
        pl.BlockSpec((None, None, 1, W_A), lambda b, t: (layer, b, 0, 0)),
        pl.BlockSpec((None, None, CONV_W - 1, W_A), lambda b, t: (layer, b, 0, 0)),
        pl.BlockSpec((None, None, H_B, DK, DV), lambda b, t: (layer, b, 0, 0, 0)),
        pl.BlockSpec((None, None, H_B, DK), lambda b, t: (layer, b, 0, 0)),
        pl.BlockSpec((None, None, 1, LANES), lambda b, t: (layer, b, 0, 0)),
    )
    aliases = {}
    if prev_states is not None:
        for k, st in enumerate(prev_states):
            in_specs.append(pl.BlockSpec(memory_space=pl.ANY))
            args.append(st)
            aliases[n_in + k] = 1 + k
    n_alias = len(args) - n_in
    scratch = [
        pltpu.VMEM((M, D_MODEL), bf16),
        pltpu.VMEM((M + SUBLANES, W_A), f32),
        pltpu.VMEM((W_A // LANES, M, LANES), f32),
        pltpu.VMEM((W_A // LANES, M, LANES), f32),
        pltpu.VMEM((M, W_A + W_B), bf16),
        pltpu.VMEM((H_B, M, DK), bf16),
        pltpu.VMEM((H_B, n_chunks, DK, CHUNK), f32),
        pltpu.VMEM((H_B, M, DV), bf16),
        pltpu.VMEM((H_B, M, DV), f32),
        pltpu.VMEM((M, LANES), f32),
        pltpu.VMEM((n_chunks, SUBLANES, CHUNK), f32),
        pltpu.VMEM((H_B, DK, LANES), f32),
    ]
    body = _skip_alias_refs(functools.partial(_prompt_layer_body, last_layer), n_in, n_alias)
    return pl.pallas_call(
        body,
        out_shape=(jax.ShapeDtypeStruct((B, S, D_MODEL), f32),) + state_shapes,
        grid=(B, S // M),
        in_specs=in_specs,
        out_specs=out_specs,
        scratch_shapes=scratch,
        input_output_aliases=aliases,
        compiler_params=pltpu.CompilerParams(
            dimension_semantics=("arbitrary", "arbitrary"),
            vmem_limit_bytes=VMEM_LIMIT_BYTES),
        name="prompt_layer_last" if last_layer else "prompt_layer",
    )(*args)


def _dec_inproj_body(x_ref, gn_ref, win_ref, u_ref, hn_s):
    @pl.when(pl.program_id(0) == 0)
    def _():
        hn_s[...] = _rms_scale(x_ref[...], gn_ref[...]).astype(bf16)
    u_ref[...] = _mm(hn_s[...], win_ref[...])


def _dec_inproj(x, W, layer):
    nb = x.shape[0]
    return pl.pallas_call(
        _dec_inproj_body,
        out_shape=jax.ShapeDtypeStruct((nb, D_IN_PAD), f32),
        grid=(D_IN_PAD // GW,),
        in_specs=[
            pl.BlockSpec((nb, D_MODEL), lambda j: (0, 0)),
            pl.BlockSpec((None, 1, D_MODEL), lambda j: (layer, 0, 0)),
            pl.BlockSpec((None, D_MODEL, GW), lambda j: (layer, 0, j)),
        ],
        out_specs=pl.BlockSpec((nb, GW), lambda j: (0, j)),
        scratch_shapes=[pltpu.VMEM((nb, D_MODEL), bf16)],
        compiler_params=pltpu.CompilerParams(dimension_semantics=("arbitrary",)),
        name="dec_inproj",
    )(x, W["g_norm"], W["win"])


def _dec_rglru_body(u_ref, h0_ref, conv0_ref, cw_ref, cb_ref, wr_ref, wi_ref, br_ref, bi_ref, lam_ref,
                    h_ref, conv_ref, mrg_ref):
    sp_all = jax.nn.softplus(-lam_ref[...])
    hist = CONV_W - 1
    for g in range(NG_A):
        cs = slice(g * GW, (g + 1) * GW)
        xa = u_ref[:, C_XA + g * GW:C_XA + (g + 1) * GW]
        za = u_ref[:, C_ZA + g * GW:C_ZA + (g + 1) * GW]
        xc = cb_ref[:, cs] + cw_ref[hist:hist + 1, cs] * xa
        for j in range(hist):
            xc = xc + cw_ref[j:j + 1, cs] * conv0_ref[:, j * W_A + g * GW:j * W_A + (g + 1) * GW]
        a, u = _rglru_gates(xc, wr_ref[g], wi_ref[g], br_ref[:, cs], bi_ref[:, cs], sp_all[:, cs])
        h = a * h0_ref[:, cs] + u
        h_ref[:, cs] = h
        mrg_ref[:, cs] = h * jax.nn.silu(za)
        for j in range(hist - 1):
            conv_ref[:, j * W_A + g * GW:j * W_A + (g + 1) * GW] = (
                conv0_ref[:, (j + 1) * W_A + g * GW:(j + 1) * W_A + (g + 1) * GW])
        conv_ref[:, (hist - 1) * W_A + g * GW:(hist - 1) * W_A + (g + 1) * GW] = xa


def _dec_rglru(u, h0_all, conv0_all, W, layer, prev_states):
    nb = u.shape[0]
    hist = CONV_W - 1
    lspec = lambda shape: pl.BlockSpec((None,) + shape, lambda i: (layer,) + (0,) * len(shape))
    in_specs = [
        pl.BlockSpec((nb, 2 * W_A), lambda i: (0, 0)),
        lspec((nb, W_A)),
        lspec((nb, hist * W_A)),
        lspec((CONV_W, W_A)), lspec((1, W_A)),
        lspec((NG_A, GW, GW)), lspec((NG_A, GW, GW)),
        lspec((1, W_A)), lspec((1, W_A)), lspec((1, W_A)),
    ]
    args = [u, h0_all, conv0_all, W["conv_w"], W["conv_b"], W["wr"], W["wi"], W["b_r"], W["b_i"], W["lam"]]
    n_in = len(args)
    aliases = {}
    if prev_states is not None:
        for k, st in enumerate(prev_states):
            in_specs.append(pl.BlockSpec(memory_space=pl.ANY))
            args.append(st)
            aliases[n_in + k] = k
    body = _skip_alias_refs(_dec_rglru_body, n_in, len(args) - n_in)
    return pl.pallas_call(
        body,
        out_shape=(
            jax.ShapeDtypeStruct((DEPTH, nb, W_A), f32),
            jax.ShapeDtypeStruct((DEPTH, nb, hist * W_A), f32),
            jax.ShapeDtypeStruct((nb, W_A), f32),
        ),
        grid=(1,),
        in_specs=in_specs,
        out_specs=(lspec((nb, W_A)), lspec((nb, hist * W_A)), pl.BlockSpec((nb, W_A), lambda i: (0, 0))),
        input_output_aliases=aliases,
        compiler_params=pltpu.CompilerParams(dimension_semantics=("arbitrary",)),
        name="dec_rglru",
    )(*args)


def _dec_mlstm_body(u_ref, c0_ref, n0_ref, m0_ref, bias_ref, gm_ref, c_ref, n_ref, m_ref, mrg_ref):
    row8 = lax.broadcasted_iota(jnp.int32, (2 * H_B, DV), 0)

    def one(bi, carry):
        urow = u_ref[bi]
        gates = urow[:, C_IF:C_IF + 2 * H_B] + bias_ref[...]
        qk = jnp.concatenate([urow[:, C_Q + j * DK:C_Q + (j + 1) * DK] for j in range(2 * H_B)], axis=0)
        cols = qk.T.astype(bf16)
        m_prev_all = m0_ref[bi]
        for h in range(H_B):
            ig = gates[:, h:h + 1]
            lf = jax.nn.log_sigmoid(gates[:, H_B + h:H_B + h + 1])
            m_prev = m_prev_all[:, h:h + 1]
            gcarry = lf + m_prev
            m_t = jnp.maximum(gcarry, ig)
            w = jnp.exp(ig - m_t)
            ginter = jnp.exp(gcarry - m_t)
            q = urow[:, C_Q + h * DK:C_Q + (h + 1) * DK]
            k = urow[:, C_K + h * DK:C_K + (h + 1) * DK] * (DK ** -0.5)
            v = urow[:, C_V + h * DV:C_V + (h + 1) * DV]
            o = urow[:, C_O + h * DV:C_O + (h + 1) * DV]
            zb = urow[:, C_ZB + h * DV:C_ZB + (h + 1) * DV]
            ch = c0_ref[bi, h]
            nh = n0_ref[bi, h:h + 1, :]
            q8 = jnp.broadcast_to(q, (SUBLANES, DK)).astype(bf16)
            qc = _mm(q8, ch.astype(bf16))[0:1, :]
            p = w * jnp.sum(q * k, axis=-1, keepdims=True)
            num = p * v + ginter * qc
            den = p + ginter * jnp.sum(q * nh, axis=-1, keepdims=True)
            denom = jnp.maximum(jnp.abs(den), jnp.exp(-m_t))
            hb = num / denom
            wv = (w * (DK ** -0.5)) * v
            rhs = jnp.where(row8 == H_B + h, jnp.broadcast_to(wv, (2 * H_B, DV)), 0.0).astype(bf16)
            c_ref[bi, h] = ginter * ch + _mm(cols, rhs)
            n_ref[bi, h:h + 1, :] = ginter * nh + w * k
            m_ref[bi, :, h:h + 1] = m_t
            mrg_ref[bi, :, h * DV:(h + 1) * DV] = _head_norm_gate(o, hb, zb, gm_ref[:, h * DV:(h + 1) * DV])
        return carry

    lax.fori_loop(0, DEC_BB, one, 0)


def _dec_mlstm(u3, c0_all, n0_all, m0_all, W, layer, prev_states):
    nb = u3.shape[0]
    in_specs = [
        pl.BlockSpec((DEC_BB, 1, D_IN_PAD), lambda i: (i, 0, 0)),
        pl.BlockSpec((None, DEC_BB, H_B, DK, DV), lambda i: (layer, i, 0, 0, 0)),
        pl.BlockSpec((None, DEC_BB, H_B, DK), lambda i: (layer, i, 0, 0)),
        pl.BlockSpec((None, DEC_BB, 1, H_B), lambda i: (layer, i, 0, 0)),
        pl.BlockSpec((None, 1, 2 * H_B), lambda i: (layer, 0, 0)),
        pl.BlockSpec((None, 1, W_B), lambda i: (layer, 0, 0)),
    ]
    args = [u3, c0_all, n0_all, m0_all, W["bias8"], W["g_mhead"]]
    n_in = len(args)
    aliases = {}
    if prev_states is not None:
        for k, st in enumerate(prev_states):
            in_specs.append(pl.BlockSpec(memory_space=pl.ANY))
            args.append(st)
            aliases[n_in + k] = k
    body = _skip_alias_refs(_dec_mlstm_body, n_in, len(args) - n_in)
    return pl.pallas_call(
        body,
        out_shape=(
            jax.ShapeDtypeStruct((DEPTH, nb, H_B, DK, DV), f32),
            jax.ShapeDtypeStruct((DEPTH, nb, H_B, DK), f32),
            jax.ShapeDtypeStruct((DEPTH, nb, 1, H_B), f32),
            jax.ShapeDtypeStruct((nb, 1, W_B), f32),
        ),
        grid=(nb // DEC_BB,),
        in_specs=in_specs,
        out_specs=(
            pl.BlockSpec((None, DEC_BB, H_B, DK, DV), lambda i: (layer, i, 0, 0, 0)),
            pl.BlockSpec((None, DEC_BB, H_B, DK), lambda i: (layer, i, 0, 0)),
            pl.BlockSpec((None, DEC_BB, 1, H_B), lambda i: (layer, i, 0, 0)),
            pl.BlockSpec((DEC_BB, 1, W_B), lambda i: (i, 0, 0)),
        ),
        input_output_aliases=aliases,
        compiler_params=pltpu.CompilerParams(dimension_semantics=("arbitrary",),
                                             vmem_limit_bytes=VMEM_LIMIT_BYTES),
        name="dec_mlstm",
    )(*args)


def _dec_out_body(last_layer, x_ref, ma_ref, mb_ref, wout_ref, gf_ref, y_ref):
    out = (x_ref[...] + _mm(ma_ref[...].astype(bf16), wout_ref[0:W_A, :])
           + _mm(mb_ref[...].astype(bf16), wout_ref[W_A:W_A + W_B, :]))
    if last_layer:
        out = _rms_scale(out, gf_ref[...])
    y_ref[...] = out


def _dec_out(x, ma, mb, W, layer):
    nb = x.shape[0]
    last_layer = layer == DEPTH - 1
    full = lambda shape: pl.BlockSpec(shape, lambda i: (0,) * len(shape))
    return pl.pallas_call(
        functools.partial(_dec_out_body, last_layer),
        out_shape=jax.ShapeDtypeStruct((nb, D_MODEL), f32),
        grid=(1,),
        in_specs=[full((nb, D_MODEL)), full((nb, W_A)), full((nb, W_B)),
                  pl.BlockSpec((None, W_A + W_B, D_MODEL), lambda i: (layer, 0, 0)), full((1, D_MODEL))],
        out_specs=full((nb, D_MODEL)),
        compiler_params=pltpu.CompilerParams(dimension_semantics=("arbitrary",)),
        name="dec_out_last" if last_layer else "dec_out",
    )(x, ma, mb, W["wout"], W["g_final"])


def _block_diag_groups(w):
    per = GW // BW_A
    wg = w.reshape(DEPTH, NG_A, per, BW_A, BW_A)
    eye = jnp.eye(per, dtype=w.dtype)
    return jnp.einsum("lgpcd,pq->lgpcqd", wg, eye).reshape(DEPTH, NG_A, GW, GW).astype(bf16)


def _prepare_weights(g_norm, w_in, conv_w, conv_b, w_rgate, b_rgate, w_igate, b_igate,
                     lru_lambda, b_mi, b_mf, g_mhead, w_out, g_final):
    d_in = w_in.shape[-1]
    win = jnp.pad(w_in.astype(bf16), ((0, 0), (0, 0), (0, D_IN_PAD - d_in)))
    wif = w_in[:, :, C_IF:C_IF + 2 * H_B]
    wk = w_in[:, :, C_K:C_K + W_B].reshape(DEPTH, D_MODEL, H_B, DK)
    bias8 = jnp.concatenate([b_mi, b_mf], axis=-1)[:, None, :]
    tril = jnp.tril(jnp.ones((CHUNK, CHUNK), f32))
    return {
        "g_norm": g_norm[:, None, :],
        "win": win,
        "wift": jnp.swapaxes(wif, 1, 2).astype(bf16),
        "wkt": jnp.transpose(wk, (0, 2, 3, 1)).astype(bf16),
        "bias8": bias8,
        "bcol": jnp.pad(bias8, ((0, 0), (0, 0), (0, LANES - 2 * H_B))),
        "brow": jnp.swapaxes(bias8, 1, 2),
        "conv_w": conv_w, "conv_b": conv_b[:, None, :],
        "wr": _block_diag_groups(w_rgate), "wi": _block_diag_groups(w_igate),
        "b_r": b_rgate[:, None, :], "b_i": b_igate[:, None, :], "lam": lru_lambda[:, None, :],
        "g_mhead": g_mhead[:, None, :],
        "wout": w_out.astype(bf16),
        "g_final": g_final[None, :],
        "tril": tril, "triu": tril.T,
    }


def kernel(x_prompt, x_sample, state_rglru_h, state_rglru_conv, state_mlstm_C, state_mlstm_n, state_mlstm_m, g_norm, w_in, conv_w, conv_b, w_rgate, b_rgate, w_igate, b_igate, lru_lambda, b_mi, b_mf, g_mhead, w_out, g_final):
    W = _prepare_weights(g_norm, w_in, conv_w, conv_b, w_rgate, b_rgate, w_igate, b_igate,
                         lru_lambda, b_mi, b_mf, g_mhead, w_out, g_final)

    x = x_prompt
    p_states = None
    for l in range(DEPTH):
        x, *p_states = _prompt_layer(x, W, l, p_states)
    p_h, p_conv, p_c, p_n, p_m = p_states

    nb = x_sample.shape[0]
    hist = CONV_W - 1
    xs = x_sample.reshape(nb, D_MODEL)
    conv0 = state_rglru_conv.reshape(DEPTH, nb, hist * W_A)
    m0 = state_mlstm_m.reshape(DEPTH, nb, 1, H_B)
    a_states, b_states = None, None
    for l in range(DEPTH):
        u = _dec_inproj(xs, W, l)
        s_h, s_conv, ma = _dec_rglru(u, state_rglru_h, conv0, W, l, a_states)
        a_states = (s_h, s_conv)
        s_c, s_n, s_m, mb = _dec_mlstm(u.reshape(nb, 1, D_IN_PAD), state_mlstm_C, state_mlstm_n, m0, W, l, b_states)
        b_states = (s_c, s_n, s_m)
        xs = _dec_out(xs, ma, mb.reshape(nb, W_B), W, l)

    return (x, xs.reshape(nb, 1, D_MODEL),
            p_h[:, :, 0, :], p_conv, p_c, p_n, p_m[:, :, 0, :H_B],
            s_h, s_conv.reshape(DEPTH, nb, hist, W_A), s_c, s_n, s_m.reshape(DEPTH, nb, H_B))
```

```python
import functools

import jax
import jax.numpy as jnp
from jax import lax
from jax.experimental import pallas as pl
from jax.experimental.pallas import tpu as pltpu

f32 = jnp.float32
bf16 = jnp.bfloat16

D_MODEL = 1024
W_A = 1024
W_B = 1024
NB_A = 16
BW_A = W_A // NB_A
CONV_W = 4
LRU_C = 8.0
H_B = 4
DK = W_B // H_B
DV = W_B // H_B
CHUNK = 128
EPS = 1e-6
DEPTH = 2

MXU_TILE = 256
LANES = 128
SUBLANES = 8
VMEM_LIMIT_BYTES = 56 * 1024 * 1024

GW = MXU_TILE
NG_A = W_A // GW
C_XA, C_ZA = 0, W_A
C_Q = 2 * W_A
C_K, C_V, C_O, C_ZB = C_Q + W_B, C_Q + 2 * W_B, C_Q + 3 * W_B, C_Q + 4 * W_B
C_IF = C_Q + 5 * W_B
D_IN_PAD = C_IF + GW

PROMPT_TILE = 512
SCAN_SHIFTS = (1, 2, 4)
DEC_BB = 4


def _mm(a, b):
    return jnp.dot(a, b, preferred_element_type=f32)


def _rms_scale(x, g):
    return x * lax.rsqrt(jnp.mean(x * x, axis=-1, keepdims=True) + EPS) * g


def _rglru_gates(xc, wr, wi, br, bi, sp):
    xcb = xc.astype(bf16)
    r = jax.nn.sigmoid(_mm(xcb, wr) + br)
    i = jax.nn.sigmoid(_mm(xcb, wi) + bi)
    log_a = (-LRU_C) * r * sp
    a = jnp.exp(log_a)
    mult = jnp.sqrt(-jnp.tanh(log_a) * (1.0 + a * a))
    return a, mult * (i * xc)


def _head_norm_gate(o, hb, zb, gm):
    yb = jax.nn.sigmoid(o) * hb
    yb = yb * lax.rsqrt(jnp.mean(yb * yb, axis=-1, keepdims=True) + EPS)
    return yb * gm * jax.nn.silu(zb)


def _twice(x):
    return jnp.concatenate([x, x], axis=1)


def _skip_alias_refs(body, n_in, n_alias):
    def wrapped(*refs):
        return body(*refs[:n_in], *refs[n_in + n_alias:])
    return wrapped


def _prompt_layer_body(last_layer,
                       x_ref, gn_ref, win_ref, wift_ref, wkt_ref, bcol_ref, brow_ref,
                       cw_ref, cb_ref, wr_ref, wi_ref, br_ref, bi_ref, lam_ref,
                       gm_ref, wout_ref, gf_ref, tril_ref, triu_ref,
                       y_ref, h_ref, conv_ref, c_ref, n_ref, m_ref,
                       hn_s, xa_s, mrg_s, q_s, kt_s, v_s, hb_s, gcol_s, grow_s, nrep_s):
    M = PROMPT_TILE
    n_chunks = M // CHUNK
    t = pl.program_id(1)
    hist = CONV_W - 1

    @pl.when(t == 0)
    def _init():
        h_ref[...] = jnp.zeros_like(h_ref)
        c_ref[...] = jnp.zeros_like(c_ref)
        m_ref[...] = jnp.zeros_like(m_ref)
        nrep_s[...] = jnp.zeros_like(nrep_s)
        xa_s[0:SUBLANES, :] = jnp.zeros((SUBLANES, W_A), f32)

    hn_s[...] = _rms_scale(x_ref[...], gn_ref[...]).astype(bf16)

    gcol_s[...] = _mm(hn_s[...], win_ref[:, C_IF:C_IF + GW])[:, 0:LANES] + bcol_ref[...]
    grow = lax.dot_general(wift_ref[...], hn_s[...], (((1,), (1,)), ((), ())),
                           preferred_element_type=f32) + brow_ref[...]
    for j in range(n_chunks):
        grow_s[j] = grow[:, j * CHUNK:(j + 1) * CHUNK]

    sp_all = jax.nn.softplus(-lam_ref[...])
    row_in_block = lax.broadcasted_iota(jnp.int32, (SUBLANES, GW), 0)
    keeps = [row_in_block >= sh for sh in SCAN_SHIFTS]
    for g in range(NG_A):
        cs = slice(g * GW, (g + 1) * GW)
        xa_s[SUBLANES:SUBLANES + M, cs] = _mm(hn_s[...], win_ref[:, C_XA + g * GW:C_XA + (g + 1) * GW])
        xc = cb_ref[:, cs] + cw_ref[hist:hist + 1, cs] * xa_s[SUBLANES:SUBLANES + M, cs]
        for j in range(hist):
            off = SUBLANES - hist + j
            xc = xc + cw_ref[j:j + 1, cs] * xa_s[off:off + M, cs]
        a, u = _rglru_gates(xc, wr_ref[g], wi_ref[g], br_ref[:, cs], bi_ref[:, cs], sp_all[:, cs])
        hprev = h_ref[:, cs]
        hblocks = []
        for b in range(M // SUBLANES):
            A = a[b * SUBLANES:(b + 1) * SUBLANES, :]
            Bv = u[b * SUBLANES:(b + 1) * SUBLANES, :]
            for sh, keep in zip(SCAN_SHIFTS, keeps):
                Bv = jnp.where(keep, A * pltpu.roll(Bv, sh, axis=0) + Bv, Bv)
                A = jnp.where(keep, A * pltpu.roll(A, sh, axis=0), A)
            hblk = A * hprev + Bv
            hprev = hblk[SUBLANES - 1:SUBLANES, :]
            hblocks.append(hblk)
        h_ref[:, cs] = hprev
        za = _mm(hn_s[...], win_ref[:, C_ZA + g * GW:C_ZA + (g + 1) * GW])
        mrg_s[:, cs] = (jnp.concatenate(hblocks, axis=0) * jax.nn.silu(za)).astype(bf16)

    @pl.when(t == pl.num_programs(1) - 1)
    def _conv_out():
        conv_ref[...] = xa_s[SUBLANES + M - hist:SUBLANES + M, :]

    xa_s[SUBLANES - hist:SUBLANES, :] = xa_s[SUBLANES + M - hist:SUBLANES + M, :]

    for h in range(H_B):
        q_s[h] = _mm(hn_s[...], win_ref[:, C_Q + h * DK:C_Q + (h + 1) * DK]).astype(bf16)
        v_s[h] = _mm(hn_s[...], win_ref[:, C_V + h * DV:C_V + (h + 1) * DV]).astype(bf16)
        kt = lax.dot_general(wkt_ref[h], hn_s[...], (((1,), (1,)), ((), ())),
                             preferred_element_type=f32) * (DK ** -0.5)
        for j in range(n_chunks):
            kt_s[h, j] = kt[:, j * CHUNK:(j + 1) * CHUNK]

    tril = tril_ref[...]
    causal = tril > 0.0

    def chunk_body(ci, m_carry):
        rows = slice(ci * CHUNK, (ci + 1) * CHUNK)
        gc = gcol_s[rows, :]
        lf_c = jax.nn.log_sigmoid(gc)
        b_c = jnp.dot(tril, lf_c, preferred_element_type=f32, precision=lax.Precision.HIGHEST)
        gr = grow_s[ci]
        lf_r = jax.nn.log_sigmoid(gr)
        b_r = jnp.dot(lf_r, triu_ref[...], preferred_element_type=f32, precision=lax.Precision.HIGHEST)
        m_new = []
        for h in range(H_B):
            m_prev = m_carry[h]
            bcol = jnp.broadcast_to(b_c[:, H_B + h:H_B + h + 1], (CHUNK, LANES))
            brow = b_r[H_B + h:H_B + h + 1, :]
            igrow = gr[h:h + 1, :]
            dmat = jnp.where(causal, bcol - brow + igrow, -jnp.inf)
            gcarry = bcol + m_prev
            m_t = jnp.maximum(gcarry, jnp.max(dmat, axis=-1, keepdims=True))
            w = jnp.exp(dmat - m_t)
            ginter = jnp.exp(gcarry - m_t)
            qh = q_s[h, rows, :]
            vh = v_s[h, rows, :]
            kth = kt_s[h, ci]
            p = w * _mm(qh, kth.astype(bf16))
            ch = c_ref[h]
            nrep = nrep_s[h]
            num = _mm(p.astype(bf16), vh) + _twice(ginter) * _mm(qh, ch.astype(bf16))
            den = jnp.sum(p, axis=-1, keepdims=True) + ginter * _mm(qh, nrep.astype(bf16))
            denom = jnp.maximum(jnp.abs(den), jnp.exp(-m_t))
            hb_s[h, rows, :] = num * _twice(1.0 / denom)
            m_last = m_t[CHUNK - 1:CHUNK, :]
            b_last = bcol[CHUNK - 1:CHUNK, :]
            decay = jnp.exp(b_last - brow + igrow - m_last)
            cscale = jnp.exp(b_last + m_prev - m_last)
            dkt = kth * decay
            c_ref[h] = _twice(cscale) * ch + _mm(dkt.astype(bf16), vh)
            nrep_s[h] = cscale * nrep + jnp.sum(dkt, axis=-1, keepdims=True)
            m_new.append(m_last)
        return tuple(m_new)

    m0 = tuple(jnp.broadcast_to(m_ref[:, h:h + 1], (1, LANES)) for h in range(H_B))
    m_fin = m0
    for ci in range(n_chunks):
        m_fin = chunk_body(ci, m_fin)
    for h in range(H_B):
        m_ref[:, h:h + 1] = m_fin[h][:, 0:1]

    @pl.when(t == pl.num_programs(1) - 1)
    def _n_out():
        for h in range(H_B):
            n_ref[h:h + 1, :] = nrep_s[h].T[0:1, :]

    for h in range(H_B):
        cs = slice(h * DV, (h + 1) * DV)
        o = _mm(hn_s[...], win_ref[:, C_O + h * DV:C_O + (h + 1) * DV])
        zb = _mm(hn_s[...], win_ref[:, C_ZB + h * DV:C_ZB + (h + 1) * DV])
        mrg_s[:, W_A + h * DV:W_A + (h + 1) * DV] = _head_norm_gate(o, hb_s[h], zb, gm_ref[:, cs]).astype(bf16)

    out = x_ref[...] + _mm(mrg_s[...], wout_ref[...])
    if last_layer:
        out = _rms_scale(out, gf_ref[...])
    y_ref[...] = out


def _prompt_layer(x, W, layer, prev_states):
    B, S, _ = x.shape
    M = PROMPT_TILE
    n_chunks = M // CHUNK
    last_layer = layer == DEPTH - 1

    def wspec(shape):
        nd = len(shape)
        return pl.BlockSpec((None,) + shape, lambda b, t: (layer,) + (0,) * nd, pipeline_mode=pl.Buffered(1))

    def cspec(shape):
        nd = len(shape)
        return pl.BlockSpec(shape, lambda b, t: (0,) * nd, pipeline_mode=pl.Buffered(1))

    in_specs = [
        pl.BlockSpec((None, M, D_MODEL), lambda b, t: (b, t, 0)),
        wspec((1, D_MODEL)),
        wspec((D_MODEL, D_IN_PAD)),
        wspec((SUBLANES, D_MODEL)),
        wspec((H_B, DK, D_MODEL)),
        wspec((1, LANES)),
        wspec((SUBLANES, 1)),
        wspec((CONV_W, W_A)),
        wspec((1, W_A)),
        wspec((NG_A, GW, GW)),
        wspec((NG_A, GW, GW)),
        wspec((1, W_A)),
        wspec((1, W_A)),
        wspec((1, W_A)),
        wspec((1, W_B)),
        wspec((W_A + W_B, D_MODEL)),
        cspec((1, D_MODEL)),
        cspec((CHUNK, CHUNK)),
        cspec((CHUNK, CHUNK)),
    ]
    args = [x, W["g_norm"], W["win"], W["wift"], W["wkt"], W["bcol"], W["brow"], W["conv_w"], W["conv_b"],
            W["wr"], W["wi"], W["b_r"], W["b_i"], W["lam"], W["g_mhead"], W["wout"], W["g_final"],
            W["tril"], W["triu"]]
    n_in = len(args)
    state_shapes = (
        jax.ShapeDtypeStruct((DEPTH, B, 1, W_A), f32),
        jax.ShapeDtypeStruct((DEPTH, B, CONV_W - 1, W_A), f32),
        jax.ShapeDtypeStruct((DEPTH, B, H_B, DK, DV), f32),
        jax.ShapeDtypeStruct((DEPTH, B, H_B, DK), f32),
        jax.ShapeDtypeStruct((DEPTH, B, 1, LANES), f32),
    )
    out_specs = (
        pl.BlockSpec((None, M, D_MODEL), lambda b, t: (b, t, 0)),
        pl.BlockSpec((None, None, 1, W_A), lambda b, t: (layer, b, 0, 0)),
        pl.BlockSpec((None, None, CONV_W - 1, W_A), lambda b, t: (layer, b, 0, 0)),
        pl.BlockSpec((None, None, H_B, DK, DV), lambda b, t: (layer, b, 0, 0, 0)),
        pl.BlockSpec((None, None, H_B, DK), lambda b, t: (layer, b, 0, 0)),
        pl.BlockSpec((None, None, 1, LANES), lambda b, t: (layer, b, 0, 0)),
    )
    aliases = {}
    if prev_states is not None:
        for k, st in enumerate(prev_states):
            in_specs.append(pl.BlockSpec(memory_space=pl.ANY))
            args.append(st)
            aliases[n_in + k] = 1 + k
    n_alias = len(args) - n_in
    scratch = [
        pltpu.VMEM((M, D_MODEL), bf16),
        pltpu.VMEM((M + SUBLANES, W_A), f32),
        pltpu.VMEM((M, W_A + W_B), bf16),
        pltpu.VMEM((H_B, M, DK), bf16),
        pltpu.VMEM((H_B, n_chunks, DK, CHUNK), f32),
        pltpu.VMEM((H_B, M, DV), bf16),
        pltpu.VMEM((H_B, M, DV), f32),
        pltpu.VMEM((M, LANES), f32),
        pltpu.VMEM((n_chunks, SUBLANES, CHUNK), f32),
        pltpu.VMEM((H_B, DK, LANES), f32),
    ]
    body = _skip_alias_refs(functools.partial(_prompt_layer_body, last_layer), n_in, n_alias)
    return pl.pallas_call(
        body,
        out_shape=(jax.ShapeDtypeStruct((B, S, D_MODEL), f32),) + state_shapes,
        grid=(B, S // M),
        in_specs=in_specs,
        out_specs=out_specs,
        scratch_shapes=scratch,
        input_output_aliases=aliases,
        compiler_params=pltpu.CompilerParams(
            dimension_semantics=("arbitrary", "arbitrary"),
            vmem_limit_bytes=VMEM_LIMIT_BYTES),
        name="prompt_layer_last" if last_layer else "prompt_layer",
    )(*args)


def _dec_inproj_body(x_ref, gn_ref, win_ref, u_ref, hn_s):
    @pl.when(pl.program_id(0) == 0)
    def _():
        hn_s[...] = _rms_scale(x_ref[...], gn_ref[...]).astype(bf16)
    u_ref[...] = _mm(hn_s[...], win_ref[...])


def _dec_inproj(x, W, layer):
    nb = x.shape[0]
    return pl.pallas_call(
        _dec_inproj_body,
        out_shape=jax.ShapeDtypeStruct((nb, D_IN_PAD), f32),
        grid=(D_IN_PAD // GW,),
        in_specs=[
            pl.BlockSpec((nb, D_MODEL), lambda j: (0, 0)),
            pl.BlockSpec((None, 1, D_MODEL), lambda j: (layer, 0, 0)),
            pl.BlockSpec((None, D_MODEL, GW), lambda j: (layer, 0, j)),
        ],
        out_specs=pl.BlockSpec((nb, GW), lambda j: (0, j)),
        scratch_shapes=[pltpu.VMEM((nb, D_MODEL), bf16)],
        compiler_params=pltpu.CompilerParams(dimension_semantics=("arbitrary",)),
        name="dec_inproj",
    )(x, W["g_norm"], W["win"])


def _dec_rglru_body(u_ref, h0_ref, conv0_ref, cw_ref, cb_ref, wr_ref, wi_ref, br_ref, bi_ref, lam_ref,
                    h_ref, conv_ref, mrg_ref):
    sp_all = jax.nn.softplus(-lam_ref[...])
    hist = CONV_W - 1
    for g in range(NG_A):
        cs = slice(g * GW, (g + 1) * GW)
        xa = u_ref[:, C_XA + g * GW:C_XA + (g + 1) * GW]
        za = u_ref[:, C_ZA + g * GW:C_ZA + (g + 1) * GW]
        xc = cb_ref[:, cs] + cw_ref[hist:hist + 1, cs] * xa
        for j in range(hist):
            xc = xc + cw_ref[j:j + 1, cs] * conv0_ref[:, j * W_A + g * GW:j * W_A + (g + 1) * GW]
        a, u = _rglru_gates(xc, wr_ref[g], wi_ref[g], br_ref[:, cs], bi_ref[:, cs], sp_all[:, cs])
        h = a * h0_ref[:, cs] + u
        h_ref[:, cs] = h
        mrg_ref[:, cs] = h * jax.nn.silu(za)
        for j in range(hist - 1):
            conv_ref[:, j * W_A + g * GW:j * W_A + (g + 1) * GW] = (
                conv0_ref[:, (j + 1) * W_A + g * GW:(j + 1) * W_A + (g + 1) * GW])
        conv_ref[:, (hist - 1) * W_A + g * GW:(hist - 1) * W_A + (g + 1) * GW] = xa


def _dec_rglru(u, h0_all, conv0_all, W, layer, prev_states):
    nb = u.shape[0]
    hist = CONV_W - 1
    lspec = lambda shape: pl.BlockSpec((None,) + shape, lambda i: (layer,) + (0,) * len(shape))
    in_specs = [
        pl.BlockSpec((nb, 2 * W_A), lambda i: (0, 0)),
        lspec((nb, W_A)),
        lspec((nb, hist * W_A)),
        lspec((CONV_W, W_A)), lspec((1, W_A)),
        lspec((NG_A, GW, GW)), lspec((NG_A, GW, GW)),
        lspec((1, W_A)), lspec((1, W_A)), lspec((1, W_A)),
    ]
    args = [u, h0_all, conv0_all, W["conv_w"], W["conv_b"], W["wr"], W["wi"], W["b_r"], W["b_i"], W["lam"]]
    n_in = len(args)
    aliases = {}
    if prev_states is not None:
        for k, st in enumerate(prev_states):
            in_specs.append(pl.BlockSpec(memory_space=pl.ANY))
            args.append(st)
            aliases[n_in + k] = k
    body = _skip_alias_refs(_dec_rglru_body, n_in, len(args) - n_in)
    return pl.pallas_call(
        body,
        out_shape=(
            jax.ShapeDtypeStruct((DEPTH, nb, W_A), f32),
            jax.ShapeDtypeStruct((DEPTH, nb, hist * W_A), f32),
            jax.ShapeDtypeStruct((nb, W_A), f32),
        ),
        grid=(1,),
        in_specs=in_specs,
        out_specs=(lspec((nb, W_A)), lspec((nb, hist * W_A)), pl.BlockSpec((nb, W_A), lambda i: (0, 0))),
        input_output_aliases=aliases,
        compiler_params=pltpu.CompilerParams(dimension_semantics=("arbitrary",)),
        name="dec_rglru",
    )(*args)


def _dec_mlstm_body(u_ref, c0_ref, n0_ref, m0_ref, bias_ref, gm_ref, c_ref, n_ref, m_ref, mrg_ref):
    row8 = lax.broadcasted_iota(jnp.int32, (2 * H_B, DV), 0)

    def one(bi, carry):
        urow = u_ref[bi]
        gates = urow[:, C_IF:C_IF + 2 * H_B] + bias_ref[...]
        qk = jnp.concatenate([urow[:, C_Q + j * DK:C_Q + (j + 1) * DK] for j in range(2 * H_B)], axis=0)
        cols = qk.T.astype(bf16)
        m_prev_all = m0_ref[bi]
        for h in range(H_B):
            ig = gates[:, h:h + 1]
            lf = jax.nn.log_sigmoid(gates[:, H_B + h:H_B + h + 1])
            m_prev = m_prev_all[:, h:h + 1]
            gcarry = lf + m_prev
            m_t = jnp.maximum(gcarry, ig)
            w = jnp.exp(ig - m_t)
            ginter = jnp.exp(gcarry - m_t)
            q = urow[:, C_Q + h * DK:C_Q + (h + 1) * DK]
            k = urow[:, C_K + h * DK:C_K + (h + 1) * DK] * (DK ** -0.5)
            v = urow[:, C_V + h * DV:C_V + (h + 1) * DV]
            o = urow[:, C_O + h * DV:C_O + (h + 1) * DV]
            zb = urow[:, C_ZB + h * DV:C_ZB + (h + 1) * DV]
            ch = c0_ref[bi, h]
            nh = n0_ref[bi, h:h + 1, :]
            q8 = jnp.broadcast_to(q, (SUBLANES, DK)).astype(bf16)
            qc = _mm(q8, ch.astype(bf16))[0:1, :]
            p = w * jnp.sum(q * k, axis=-1, keepdims=True)
            num = p * v + ginter * qc
            den = p + ginter * jnp.sum(q * nh, axis=-1, keepdims=True)
            denom = jnp.maximum(jnp.abs(den), jnp.exp(-m_t))
            hb = num / denom
            wv = (w * (DK ** -0.5)) * v
            rhs = jnp.where(row8 == H_B + h, jnp.broadcast_to(wv, (2 * H_B, DV)), 0.0).astype(bf16)
            c_ref[bi, h] = ginter * ch + _mm(cols, rhs)
            n_ref[bi, h:h + 1, :] = ginter * nh + w * k
            m_ref[bi, :, h:h + 1] = m_t
            mrg_ref[bi, :, h * DV:(h + 1) * DV] = _head_norm_gate(o, hb, zb, gm_ref[:, h * DV:(h + 1) * DV])
        return carry

    lax.fori_loop(0, DEC_BB, one, 0)


def _dec_mlstm(u3, c0_all, n0_all, m0_all, W, layer, prev_states):
    nb = u3.shape[0]
    in_specs = [
        pl.BlockSpec((DEC_BB, 1, D_IN_PAD), lambda i: (i, 0, 0)),
        pl.BlockSpec((None, DEC_BB, H_B, DK, DV), lambda i: (layer, i, 0, 0, 0)),
        pl.BlockSpec((None, DEC_BB, H_B, DK), lambda i: (layer, i, 0, 0)),
        pl.BlockSpec((None, DEC_BB, 1, H_B), lambda i: (layer, i, 0, 0)),
        pl.BlockSpec((None, 1, 2 * H_B), lambda i: (layer, 0, 0)),
        pl.BlockSpec((None, 1, W_B), lambda i: (layer, 0, 0)),
    ]
    args = [u3, c0_all, n0_all, m0_all, W["bias8"], W["g_mhead"]]
    n_in = len(args)
    aliases = {}
    if prev_states is not None:
        for k, st in enumerate(prev_states):
            in_specs.append(pl.BlockSpec(memory_space=pl.ANY))
            args.append(st)
            aliases[n_in + k] = k
    body = _skip_alias_refs(_dec_mlstm_body, n_in, len(args) - n_in)
    return pl.pallas_call(
        body,
        out_shape=(
            jax.ShapeDtypeStruct((DEPTH, nb, H_B, DK, DV), f32),
            jax.ShapeDtypeStruct((DEPTH, nb, H_B, DK), f32),
            jax.ShapeDtypeStruct((DEPTH, nb, 1, H_B), f32),
            jax.ShapeDtypeStruct((nb, 1, W_B), f32),
        ),
        grid=(nb // DEC_BB,),
        in_specs=in_specs,
        out_specs=(
            pl.BlockSpec((None, DEC_BB, H_B, DK, DV), lambda i: (layer, i, 0, 0, 0)),
            pl.BlockSpec((None, DEC_BB, H_B, DK), lambda i: (layer, i, 0, 0)),
            pl.BlockSpec((None, DEC_BB, 1, H_B), lambda i: (layer, i, 0, 0)),
            pl.BlockSpec((DEC_BB, 1, W_B), lambda i: (i, 0, 0)),
        ),
        input_output_aliases=aliases,
        compiler_params=pltpu.CompilerParams(dimension_semantics=("arbitrary",),
                                             vmem_limit_bytes=VMEM_LIMIT_BYTES),
        name="dec_mlstm",
    )(*args)


def _dec_out_body(last_layer, x_ref, ma_ref, mb_ref, wout_ref, gf_ref, y_ref):
    out = (x_ref[...] + _mm(ma_ref[...].astype(bf16), wout_ref[0:W_A, :])
           + _mm(mb_ref[...].astype(bf16), wout_ref[W_A:W_A + W_B, :]))
    if last_layer:
        out = _rms_scale(out, gf_ref[...])
    y_ref[...] = out


def _dec_out(x, ma, mb, W, layer):
    nb = x.shape[0]
    last_layer = layer == DEPTH - 1
    full = lambda shape: pl.BlockSpec(shape, lambda i: (0,) * len(shape))
    return pl.pallas_call(
        functools.partial(_dec_out_body, last_layer),
        out_shape=jax.ShapeDtypeStruct((nb, D_MODEL), f32),
        grid=(1,),
        in_specs=[full((nb, D_MODEL)), full((nb, W_A)), full((nb, W_B)),
                  pl.BlockSpec((None, W_A + W_B, D_MODEL), lambda i: (layer, 0, 0)), full((1, D_MODEL))],
        out_specs=full((nb, D_MODEL)),
        compiler_params=pltpu.CompilerParams(dimension_semantics=("arbitrary",)),
        name="dec_out_last" if last_layer else "dec_out",
    )(x, ma, mb, W["wout"], W["g_final"])


def _block_diag_groups(w):
    per = GW // BW_A
    wg = w.reshape(DEPTH, NG_A, per, BW_A, BW_A)
    eye = jnp.eye(per, dtype=w.dtype)
    return jnp.einsum("lgpcd,pq->lgpcqd", wg, eye).reshape(DEPTH, NG_A, GW, GW).astype(bf16)


def _prepare_weights(g_norm, w_in, conv_w, conv_b, w_rgate, b_rgate, w_igate, b_igate,
                     lru_lambda, b_mi, b_mf, g_mhead, w_out, g_final):
    d_in = w_in.shape[-1]
    win = jnp.pad(w_in.astype(bf16), ((0, 0), (0, 0), (0, D_IN_PAD - d_in)))
    wif = w_in[:, :, C_IF:C_IF + 2 * H_B]
    wk = w_in[:, :, C_K:C_K + W_B].reshape(DEPTH, D_MODEL, H_B, DK)
    bias8 = jnp.concatenate([b_mi, b_mf], axis=-1)[:, None, :]
    tril = jnp.tril(jnp.ones((CHUNK, CHUNK), f32))
    return {
        "g_norm": g_norm[:, None, :],
        "win": win,
        "wift": jnp.swapaxes(wif, 1, 2).astype(bf16),
        "wkt": jnp.transpose(wk, (0, 2, 3, 1)).astype(bf16),
        "bias8": bias8,
        "bcol": jnp.pad(bias8, ((0, 0), (0, 0), (0, LANES - 2 * H_B))),
        "brow": jnp.swapaxes(bias8, 1, 2),
        "conv_w": conv_w, "conv_b": conv_b[:, None, :],
        "wr": _block_diag_groups(w_rgate), "wi": _block_diag_groups(w_igate),
        "b_r": b_rgate[:, None, :], "b_i": b_igate[:, None, :], "lam": lru_lambda[:, None, :],
        "g_mhead": g_mhead[:, None, :],
        "wout": w_out.astype(bf16),
        "g_final": g_final[None, :],
        "tril": tril, "triu": tril.T,
    }


def kernel(x_prompt, x_sample, state_rglru_h, state_rglru_conv, state_mlstm_C, state_mlstm_n, state_mlstm_m, g_norm, w_in, conv_w, conv_b, w_rgate, b_rgate, w_igate, b_igate, lru_lambda, b_mi, b_mf, g_mhead, w_out, g_final):
    W = _prepare_weights(g_norm, w_in, conv_w, conv_b, w_rgate, b_rgate, w_igate, b_igate,
                         lru_lambda, b_mi, b_mf, g_mhead, w_out, g_final)

    x = x_prompt
    p_states = None
    for l in range(DEPTH):
        x, *p_states = _prompt_layer(x, W, l, p_states)
    p_h, p_conv, p_c, p_n, p_m = p_states

    nb = x_sample.shape[0]
    hist = CONV_W - 1
    xs = x_sample.reshape(nb, D_MODEL)
    conv0 = state_rglru_conv.reshape(DEPTH, nb, hist * W_A)
    m0 = state_mlstm_m.reshape(DEPTH, nb, 1, H_B)
    a_states, b_states = None, None
    for l in range(DEPTH):
        u = _dec_inproj(xs, W, l)
        s_h, s_conv, ma = _dec_rglru(u, state_rglru_h, conv0, W, l, a_states)
        a_states = (s_h, s_conv)
        s_c, s_n, s_m, mb = _dec_mlstm(u.reshape(nb, 1, D_IN_PAD), state_mlstm_C, state_mlstm_n, m0, W, l, b_states)
        b_states = (s_c, s_n, s_m)
        xs = _dec_out(xs, ma, mb.reshape(nb, W_B), W, l)

    return (x, xs.reshape(nb, 1, D_MODEL),
            p_h[:, :, 0, :], p_conv, p_c, p_n, p_m[:, :, 0, :H_B],
            s_h, s_conv.reshape(DEPTH, nb, hist, W_A), s_c, s_n, s_m.reshape(DEPTH, nb, H_B))
```

```python
import functools

import jax
import jax.numpy as jnp
from jax import lax
from jax.experimental import pallas as pl
from jax.experimental.pallas import tpu as pltpu

f32 = jnp.float32
bf16 = jnp.bfloat16

D_MODEL = 1024
W_A = 1024
W_B = 1024
NB_A = 16
BW_A = W_A // NB_A
CONV_W = 4
LRU_C = 8.0
H_B = 4
DK = W_B // H_B
DV = W_B // H_B
CHUNK = 128
EPS = 1e-6
DEPTH = 2

MXU_TILE = 256
LANES = 128
SUBLANES = 8
VMEM_LIMIT_BYTES = 56 * 1024 * 1024

GW = MXU_TILE
NG_A = W_A // GW
C_XA, C_ZA = 0, W_A
C_Q = 2 * W_A
C_K, C_V, C_O, C_ZB = C_Q + W_B, C_Q + 2 * W_B, C_Q + 3 * W_B, C_Q + 4 * W_B
C_IF = C_Q + 5 * W_B
D_IN_PAD = C_IF + GW

PROMPT_TILE = 512
SCAN_SHIFTS = (1, 2, 4)
DEC_BB = 4


def _mm(a, b):
    return jnp.dot(a, b, preferred_element_type=f32)


def _rms_scale(x, g):
    return x * lax.rsqrt(jnp.mean(x * x, axis=-1, keepdims=True) + EPS) * g


def _rglru_gates(xc, wr, wi, br, bi, sp):
    xcb = xc.astype(bf16)
    r = jax.nn.sigmoid(_mm(xcb, wr) + br)
    i = jax.nn.sigmoid(_mm(xcb, wi) + bi)
    log_a = (-LRU_C) * r * sp
    a = jnp.exp(log_a)
    mult = jnp.sqrt(-jnp.tanh(log_a) * (1.0 + a * a))
    return a, mult * (i * xc)


def _head_norm_gate(o, hb, zb, gm):
    yb = jax.nn.sigmoid(o) * hb
    yb = yb * lax.rsqrt(jnp.mean(yb * yb, axis=-1, keepdims=True) + EPS)
    return yb * gm * jax.nn.silu(zb)


def _twice(x):
    return jnp.concatenate([x, x], axis=1)


def _skip_alias_refs(body, n_in, n_alias):
    def wrapped(*refs):
        return body(*refs[:n_in], *refs[n_in + n_alias:])
    return wrapped


def _prompt_layer_body(last_layer,
                       x_ref, gn_ref, win_ref, wift_ref, wkt_ref, brow_ref,
                       cw_ref, cb_ref, wr_ref, wi_ref, br_ref, bi_ref, lam_ref,
                       gm_ref, wout_ref, gf_ref,
                       y_ref, h_ref, conv_ref, c_ref, n_ref, m_ref,
                       hn_s, xa_s, mrg_s, q_s, kt_s, v_s, hb_s, grow_s, nrep_s):
    M = PROMPT_TILE
    n_chunks = M // CHUNK
    t = pl.program_id(1)
    hist = CONV_W - 1

    @pl.when(t == 0)
    def _init():
        h_ref[...] = jnp.zeros_like(h_ref)
        c_ref[...] = jnp.zeros_like(c_ref)
        m_ref[...] = jnp.zeros_like(m_ref)
        nrep_s[...] = jnp.zeros_like(nrep_s)
        xa_s[0:SUBLANES, :] = jnp.zeros((SUBLANES, W_A), f32)

    hn_s[...] = _rms_scale(x_ref[...], gn_ref[...]).astype(bf16)

    grow = lax.dot_general(wift_ref[...], hn_s[...], (((1,), (1,)), ((), ())),
                           preferred_element_type=f32) + brow_ref[...]
    bsum = jax.nn.log_sigmoid(grow)
    lane_in_chunk = lax.broadcasted_iota(jnp.int32, (2 * H_B, M), 1) % CHUNK
    sh = 1
    while sh < CHUNK:
        bsum = bsum + jnp.where(lane_in_chunk >= sh, pltpu.roll(bsum, sh, axis=1), 0.0)
        sh *= 2
    is_forget_row = lax.broadcasted_iota(jnp.int32, (2 * H_B, M), 0) >= H_B
    gates = jnp.where(is_forget_row, bsum, grow)
    for j in range(n_chunks):
        grow_s[j] = gates[:, j * CHUNK:(j + 1) * CHUNK]

    sp_all = jax.nn.softplus(-lam_ref[...])
    row_in_block = lax.broadcasted_iota(jnp.int32, (SUBLANES, GW), 0)
    keeps = [row_in_block >= sh for sh in SCAN_SHIFTS]
    def proj_q(h):
        q_s[h] = _mm(hn_s[...], win_ref[:, C_Q + h * DK:C_Q + (h + 1) * DK]).astype(bf16)

    def proj_v(h):
        v_s[h] = _mm(hn_s[...], win_ref[:, C_V + h * DV:C_V + (h + 1) * DV]).astype(bf16)

    def proj_kt(h):
        kt = lax.dot_general(wkt_ref[h], hn_s[...], (((1,), (1,)), ((), ())),
                             preferred_element_type=f32) * (DK ** -0.5)
        for j in range(n_chunks):
            kt_s[h, j] = kt[:, j * CHUNK:(j + 1) * CHUNK]

    assert NG_A == H_B
    for g in range(NG_A):
        cs = slice(g * GW, (g + 1) * GW)
        xa_s[SUBLANES:SUBLANES + M, cs] = _mm(hn_s[...], win_ref[:, C_XA + g * GW:C_XA + (g + 1) * GW])
        proj_q(g)
        xc = cb_ref[:, cs] + cw_ref[hist:hist + 1, cs] * xa_s[SUBLANES:SUBLANES + M, cs]
        for j in range(hist):
            off = SUBLANES - hist + j
            xc = xc + cw_ref[j:j + 1, cs] * xa_s[off:off + M, cs]
        a, u = _rglru_gates(xc, wr_ref[g], wi_ref[g], br_ref[:, cs], bi_ref[:, cs], sp_all[:, cs])
        proj_v(g)
        hprev = h_ref[:, cs]
        hblocks = []
        for b in range(M // SUBLANES):
            A = a[b * SUBLANES:(b + 1) * SUBLANES, :]
            Bv = u[b * SUBLANES:(b + 1) * SUBLANES, :]
            for sh, keep in zip(SCAN_SHIFTS, keeps):
                Bv = jnp.where(keep, A * pltpu.roll(Bv, sh, axis=0) + Bv, Bv)
                A = jnp.where(keep, A * pltpu.roll(A, sh, axis=0), A)
            hblk = A * hprev + Bv
            hprev = hblk[SUBLANES - 1:SUBLANES, :]
            hblocks.append(hblk)
        h_ref[:, cs] = hprev
        proj_kt(g)
        za = _mm(hn_s[...], win_ref[:, C_ZA + g * GW:C_ZA + (g + 1) * GW])
        mrg_s[:, cs] = (jnp.concatenate(hblocks, axis=0) * jax.nn.silu(za)).astype(bf16)

    @pl.when(t == pl.num_programs(1) - 1)
    def _conv_out():
        conv_ref[...] = xa_s[SUBLANES + M - hist:SUBLANES + M, :]

    xa_s[SUBLANES - hist:SUBLANES, :] = xa_s[SUBLANES + M - hist:SUBLANES + M, :]

    causal = (lax.broadcasted_iota(jnp.int32, (CHUNK, CHUNK), 0)
              >= lax.broadcasted_iota(jnp.int32, (CHUNK, CHUNK), 1))

    def chunk_body(ci, m_carry):
        rows = slice(ci * CHUNK, (ci + 1) * CHUNK)
        gr = grow_s[ci]
        gr_t = gr.T
        m_new = []
        for h in range(H_B):
            m_prev = m_carry[h]
            bcol = jnp.broadcast_to(gr_t[:, H_B + h:H_B + h + 1], (CHUNK, LANES))
            brow = gr[H_B + h:H_B + h + 1, :]
            igrow = gr[h:h + 1, :]
            dmat = jnp.where(causal, bcol - brow + igrow, -jnp.inf)
            gcarry = bcol + m_prev
            m_t = jnp.maximum(gcarry, jnp.max(dmat, axis=-1, keepdims=True))
            w = jnp.exp(dmat - m_t)
            ginter = jnp.exp(gcarry - m_t)
            qh = q_s[h, rows, :]
            vh = v_s[h, rows, :]
            kth = kt_s[h, ci]
            p = w * _mm(qh, kth.astype(bf16))
            ch = c_ref[h]
            nrep = nrep_s[h]
            num = _mm(p.astype(bf16), vh) + _twice(ginter) * _mm(qh, ch.astype(bf16))
            den = jnp.sum(p, axis=-1, keepdims=True) + ginter * _mm(qh, nrep.astype(bf16))
            denom = jnp.maximum(jnp.abs(den), jnp.exp(-m_t))
            hb_s[h, rows, :] = num * _twice(1.0 / denom)
            m_last = m_t[CHUNK - 1:CHUNK, :]
            b_last = bcol[CHUNK - 1:CHUNK, :]
            decay = jnp.exp(b_last - brow + igrow - m_last)
            cscale = jnp.exp(b_last + m_prev - m_last)
            dkt = kth * decay
            c_ref[h] = _twice(cscale) * ch + _mm(dkt.astype(bf16), vh)
            nrep_s[h] = cscale * nrep + jnp.sum(dkt, axis=-1, keepdims=True)
            m_new.append(m_last)
        return tuple(m_new)

    m0 = tuple(jnp.broadcast_to(m_ref[:, h:h + 1], (1, LANES)) for h in range(H_B))
    m_fin = m0
    for ci in range(n_chunks):
        m_fin = chunk_body(ci, m_fin)
    for h in range(H_B):
        m_ref[:, h:h + 1] = m_fin[h][:, 0:1]

    @pl.when(t == pl.num_programs(1) - 1)
    def _n_out():
        for h in range(H_B):
            n_ref[h:h + 1, :] = nrep_s[h].T[0:1, :]

    for h in range(H_B):
        cs = slice(h * DV, (h + 1) * DV)
        o = _mm(hn_s[...], win_ref[:, C_O + h * DV:C_O + (h + 1) * DV])
        zb = _mm(hn_s[...], win_ref[:, C_ZB + h * DV:C_ZB + (h + 1) * DV])
        mrg_s[:, W_A + h * DV:W_A + (h + 1) * DV] = _head_norm_gate(o, hb_s[h], zb, gm_ref[:, cs]).astype(bf16)

    out = x_ref[...] + _mm(mrg_s[...], wout_ref[...])
    if last_layer:
        out = _rms_scale(out, gf_ref[...])
    y_ref[...] = out


def _prompt_layer(x, W, layer, prev_states):
    B, S, _ = x.shape
    M = PROMPT_TILE
    n_chunks = M // CHUNK
    last_layer = layer == DEPTH - 1

    def wspec(shape):
        nd = len(shape)
        return pl.BlockSpec((None,) + shape, lambda b, t: (layer,) + (0,) * nd, pipeline_mode=pl.Buffered(1))

    def cspec(shape):
        nd = len(shape)
        return pl.BlockSpec(shape, lambda b, t: (0,) * nd, pipeline_mode=pl.Buffered(1))

    in_specs = [
        pl.BlockSpec((None, M, D_MODEL), lambda b, t: (b, t, 0)),
        wspec((1, D_MODEL)),
        wspec((D_MODEL, D_IN_PAD)),
        wspec((SUBLANES, D_MODEL)),
        wspec((H_B, DK, D_MODEL)),
        wspec((SUBLANES, 1)),
        wspec((CONV_W, W_A)),
        wspec((1, W_A)),
        wspec((NG_A, GW, GW)),
        wspec((NG_A, GW, GW)),
        wspec((1, W_A)),
        wspec((1, W_A)),
        wspec((1, W_A)),
        wspec((1, W_B)),
        wspec((W_A + W_B, D_MODEL)),
        cspec((1, D_MODEL)),
    ]
    args = [x, W["g_norm"], W["win"], W["wift"], W["wkt"], W["brow"], W["conv_w"], W["conv_b"],
            W["wr"], W["wi"], W["b_r"], W["b_i"], W["lam"], W["g_mhead"], W["wout"], W["g_final"]]
    n_in = len(args)
    state_shapes = (
        jax.ShapeDtypeStruct((DEPTH, B, 1, W_A), f32),
        jax.ShapeDtypeStruct((DEPTH, B, CONV_W - 1, W_A), f32),
        jax.ShapeDtypeStruct((DEPTH, B, H_B, DK, DV), f32),
        jax.ShapeDtypeStruct((DEPTH, B, H_B, DK), f32),
        jax.ShapeDtypeStruct((DEPTH, B, 1, LANES), f32),
    )
    out_specs = (
        pl.BlockSpec((None, M, D_MODEL), lambda b, t: (b, t, 0)),
        pl.BlockSpec((None, None, 1, W_A), lambda b, t: (layer, b, 0, 0)),
        pl.BlockSpec((None, None, CONV_W - 1, W_A), lambda b, t: (layer, b, 0, 0)),
        pl.BlockSpec((None, None, H_B, DK, DV), lambda b, t: (layer, b, 0, 0, 0)),
        pl.BlockSpec((None, None, H_B, DK), lambda b, t: (layer, b, 0, 0)),
        pl.BlockSpec((None, None, 1, LANES), lambda b, t: (layer, b, 0, 0)),
    )
    aliases = {}
    if prev_states is not None:
        for k, st in enumerate(prev_states):
            in_specs.append(pl.BlockSpec(memory_space=pl.ANY))
            args.append(st)
            aliases[n_in + k] = 1 + k
    n_alias = len(args) - n_in
    scratch = [
        pltpu.VMEM((M, D_MODEL), bf16),
        pltpu.VMEM((M + SUBLANES, W_A), f32),
        pltpu.VMEM((M, W_A + W_B), bf16),
        pltpu.VMEM((H_B, M, DK), bf16),
        pltpu.VMEM((H_B, n_chunks, DK, CHUNK), f32),
        pltpu.VMEM((H_B, M, DV), bf16),
        pltpu.VMEM((H_B, M, DV), f32),
        pltpu.VMEM((n_chunks, SUBLANES, CHUNK), f32),
        pltpu.VMEM((H_B, DK, LANES), f32),
    ]
    body = _skip_alias_refs(functools.partial(_prompt_layer_body, last_layer), n_in, n_alias)
    return pl.pallas_call(
        body,
        out_shape=(jax.ShapeDtypeStruct((B, S, D_MODEL), f32),) + state_shapes,
        grid=(B, S // M),
        in_specs=in_specs,
        out_specs=out_specs,
        scratch_shapes=scratch,
        input_output_aliases=aliases,
        compiler_params=pltpu.CompilerParams(
            dimension_semantics=("arbitrary", "arbitrary"),
            vmem_limit_bytes=VMEM_LIMIT_BYTES),
        name="prompt_layer_last" if last_layer else "prompt_layer",
    )(*args)


def _dec_inproj_body(x_ref, gn_ref, win_ref, u_ref, hn_s):
    @pl.when(pl.program_id(0) == 0)
    def _():
        hn_s[...] = _rms_scale(x_ref[...], gn_ref[...]).astype(bf16)
    u_ref[...] = _mm(hn_s[...], win_ref[...])


def _dec_inproj(x, W, layer):
    nb = x.shape[0]
    return pl.pallas_call(
        _dec_inproj_body,
        out_shape=jax.ShapeDtypeStruct((nb, D_IN_PAD), f32),
        grid=(D_IN_PAD // GW,),
        in_specs=[
            pl.BlockSpec((nb, D_MODEL), lambda j: (0, 0)),
            pl.BlockSpec((None, 1, D_MODEL), lambda j: (layer, 0, 0)),
            pl.BlockSpec((None, D_MODEL, GW), lambda j: (layer, 0, j)),
        ],
        out_specs=pl.BlockSpec((nb, GW), lambda j: (0, j)),
        scratch_shapes=[pltpu.VMEM((nb, D_MODEL), bf16)],
        compiler_params=pltpu.CompilerParams(dimension_semantics=("arbitrary",)),
        name="dec_inproj",
    )(x, W["g_norm"], W["win"])


def _dec_rglru_body(u_ref, h0_ref, conv0_ref, cw_ref, cb_ref, wr_ref, wi_ref, br_ref, bi_ref, lam_ref,
                    h_ref, conv_ref, mrg_ref):
    sp_all = jax.nn.softplus(-lam_ref[...])
    hist = CONV_W - 1
    for g in range(NG_A):
        cs = slice(g * GW, (g + 1) * GW)
        xa = u_ref[:, C_XA + g * GW:C_XA + (g + 1) * GW]
        za = u_ref[:, C_ZA + g * GW:C_ZA + (g + 1) * GW]
        xc = cb_ref[:, cs] + cw_ref[hist:hist + 1, cs] * xa
        for j in range(hist):
            xc = xc + cw_ref[j:j + 1, cs] * conv0_ref[:, j * W_A + g * GW:j * W_A + (g + 1) * GW]
        a, u = _rglru_gates(xc, wr_ref[g], wi_ref[g], br_ref[:, cs], bi_ref[:, cs], sp_all[:, cs])
        h = a * h0_ref[:, cs] + u
        h_ref[:, cs] = h
        mrg_ref[:, cs] = h * jax.nn.silu(za)
        for j in range(hist - 1):
            conv_ref[:, j * W_A + g * GW:j * W_A + (g + 1) * GW] = (
                conv0_ref[:, (j + 1) * W_A + g * GW:(j + 1) * W_A + (g + 1) * GW])
        conv_ref[:, (hist - 1) * W_A + g * GW:(hist - 1) * W_A + (g + 1) * GW] = xa


def _dec_rglru(u, h0_all, conv0_all, W, layer, prev_states):
    nb = u.shape[0]
    hist = CONV_W - 1
    lspec = lambda shape: pl.BlockSpec((None,) + shape, lambda i: (layer,) + (0,) * len(shape))
    in_specs = [
        pl.BlockSpec((nb, 2 * W_A), lambda i: (0, 0)),
        lspec((nb, W_A)),
        lspec((nb, hist * W_A)),
        lspec((CONV_W, W_A)), lspec((1, W_A)),
        lspec((NG_A, GW, GW)), lspec((NG_A, GW, GW)),
        lspec((1, W_A)), lspec((1, W_A)), lspec((1, W_A)),
    ]
    args = [u, h0_all, conv0_all, W["conv_w"], W["conv_b"], W["wr"], W["wi"], W["b_r"], W["b_i"], W["lam"]]
    n_in = len(args)
    aliases = {}
    if prev_states is not None:
        for k, st in enumerate(prev_states):
            in_specs.append(pl.BlockSpec(memory_space=pl.ANY))
            args.append(st)
            aliases[n_in + k] = k
    body = _skip_alias_refs(_dec_rglru_body, n_in, len(args) - n_in)
    return pl.pallas_call(
        body,
        out_shape=(
            jax.ShapeDtypeStruct((DEPTH, nb, W_A), f32),
            jax.ShapeDtypeStruct((DEPTH, nb, hist * W_A), f32),
            jax.ShapeDtypeStruct((nb, W_A), f32),
        ),
        grid=(1,),
        in_specs=in_specs,
        out_specs=(lspec((nb, W_A)), lspec((nb, hist * W_A)), pl.BlockSpec((nb, W_A), lambda i: (0, 0))),
        input_output_aliases=aliases,
        compiler_params=pltpu.CompilerParams(dimension_semantics=("arbitrary",)),
        name="dec_rglru",
    )(*args)


def _dec_mlstm_body(u_ref, c0_ref, n0_ref, m0_ref, bias_ref, gm_ref, c_ref, n_ref, m_ref, mrg_ref):
    row8 = lax.broadcasted_iota(jnp.int32, (2 * H_B, DV), 0)

    def one(bi, carry):
        urow = u_ref[bi]
        gates = urow[:, C_IF:C_IF + 2 * H_B] + bias_ref[...]
        qk = jnp.concatenate([urow[:, C_Q + j * DK:C_Q + (j + 1) * DK] for j in range(2 * H_B)], axis=0)
        cols = qk.T.astype(bf16)
        m_prev_all = m0_ref[bi]
        for h in range(H_B):
            ig = gates[:, h:h + 1]
            lf = jax.nn.log_sigmoid(gates[:, H_B + h:H_B + h + 1])
            m_prev = m_prev_all[:, h:h + 1]
            gcarry = lf + m_prev
            m_t = jnp.maximum(gcarry, ig)
            w = jnp.exp(ig - m_t)
            ginter = jnp.exp(gcarry - m_t)
            q = urow[:, C_Q + h * DK:C_Q + (h + 1) * DK]
            k = urow[:, C_K + h * DK:C_K + (h + 1) * DK] * (DK ** -0.5)
            v = urow[:, C_V + h * DV:C_V + (h + 1) * DV]
            o = urow[:, C_O + h * DV:C_O + (h + 1) * DV]
            zb = urow[:, C_ZB + h * DV:C_ZB + (h + 1) * DV]
            ch = c0_ref[bi, h]
            nh = n0_ref[bi, h:h + 1, :]
            q8 = jnp.broadcast_to(q, (SUBLANES, DK)).astype(bf16)
            qc = _mm(q8, ch.astype(bf16))[0:1, :]
            p = w * jnp.sum(q * k, axis=-1, keepdims=True)
            num = p * v + ginter * qc
            den = p + ginter * jnp.sum(q * nh, axis=-1, keepdims=True)
            denom = jnp.maximum(jnp.abs(den), jnp.exp(-m_t))
            hb = num / denom
            wv = (w * (DK ** -0.5)) * v
            rhs = jnp.where(row8 == H_B + h, jnp.broadcast_to(wv, (2 * H_B, DV)), 0.0).astype(bf16)
            c_ref[bi, h] = ginter * ch + _mm(cols, rhs)
            n_ref[bi, h:h + 1, :] = ginter * nh + w * k
            m_ref[bi, :, h:h + 1] = m_t
            mrg_ref[bi, :, h * DV:(h + 1) * DV] = _head_norm_gate(o, hb, zb, gm_ref[:, h * DV:(h + 1) * DV])
        return carry

    lax.fori_loop(0, DEC_BB, one, 0)


def _dec_mlstm(u3, c0_all, n0_all, m0_all, W, layer, prev_states):
    nb = u3.shape[0]
    in_specs = [
        pl.BlockSpec((DEC_BB, 1, D_IN_PAD), lambda i: (i, 0, 0)),
        pl.BlockSpec((None, DEC_BB, H_B, DK, DV), lambda i: (layer, i, 0, 0, 0)),
        pl.BlockSpec((None, DEC_BB, H_B, DK), lambda i: (layer, i, 0, 0)),
        pl.BlockSpec((None, DEC_BB, 1, H_B), lambda i: (layer, i, 0, 0)),
        pl.BlockSpec((None, 1, 2 * H_B), lambda i: (layer, 0, 0)),
        pl.BlockSpec((None, 1, W_B), lambda i: (layer, 0, 0)),
    ]
    args = [u3, c0_all, n0_all, m0_all, W["bias8"], W["g_mhead"]]
    n_in = len(args)
    aliases = {}
    if prev_states is not None:
        for k, st in enumerate(prev_states):
            in_specs.append(pl.BlockSpec(memory_space=pl.ANY))
            args.append(st)
            aliases[n_in + k] = k
    body = _skip_alias_refs(_dec_mlstm_body, n_in, len(args) - n_in)
    return pl.pallas_call(
        body,
        out_shape=(
            jax.ShapeDtypeStruct((DEPTH, nb, H_B, DK, DV), f32),
            jax.ShapeDtypeStruct((DEPTH, nb, H_B, DK), f32),
            jax.ShapeDtypeStruct((DEPTH, nb, 1, H_B), f32),
            jax.ShapeDtypeStruct((nb, 1, W_B), f32),
        ),
        grid=(nb // DEC_BB,),
        in_specs=in_specs,
        out_specs=(
            pl.BlockSpec((None, DEC_BB, H_B, DK, DV), lambda i: (layer, i, 0, 0, 0)),
            pl.BlockSpec((None, DEC_BB, H_B, DK), lambda i: (layer, i, 0, 0)),
            pl.BlockSpec((None, DEC_BB, 1, H_B), lambda i: (layer, i, 0, 0)),
            pl.BlockSpec((DEC_BB, 1, W_B), lambda i: (i, 0, 0)),
        ),
        input_output_aliases=aliases,
        compiler_params=pltpu.CompilerParams(dimension_semantics=("arbitrary",),
                                             vmem_limit_bytes=VMEM_LIMIT_BYTES),
        name="dec_mlstm",
    )(*args)


def _dec_out_body(last_layer, x_ref, ma_ref, mb_ref, wout_ref, gf_ref, y_ref):
    out = (x_ref[...] + _mm(ma_ref[...].astype(bf16), wout_ref[0:W_A, :])
           + _mm(mb_ref[...].astype(bf16), wout_ref[W_A:W_A + W_B, :]))
    if last_layer:
        out = _rms_scale(out, gf_ref[...])
    y_ref[...] = out


def _dec_out(x, ma, mb, W, layer):
    nb = x.shape[0]
    last_layer = layer == DEPTH - 1
    full = lambda shape: pl.BlockSpec(shape, lambda i: (0,) * len(shape))
    return pl.pallas_call(
        functools.partial(_dec_out_body, last_layer),
        out_shape=jax.ShapeDtypeStruct((nb, D_MODEL), f32),
        grid=(1,),
        in_specs=[full((nb, D_MODEL)), full((nb, W_A)), full((nb, W_B)),
                  pl.BlockSpec((None, W_A + W_B, D_MODEL), lambda i: (layer, 0, 0)), full((1, D_MODEL))],
        out_specs=full((nb, D_MODEL)),
        compiler_params=pltpu.CompilerParams(dimension_semantics=("arbitrary",)),
        name="dec_out_last" if last_layer else "dec_out",
    )(x, ma, mb, W["wout"], W["g_final"])


def _block_diag_groups(w):
    per = GW // BW_A
    wg = w.reshape(DEPTH, NG_A, per, BW_A, BW_A)
    eye = jnp.eye(per, dtype=w.dtype)
    return jnp.einsum("lgpcd,pq->lgpcqd", wg, eye).reshape(DEPTH, NG_A, GW, GW).astype(bf16)


def _prepare_weights(g_norm, w_in, conv_w, conv_b, w_rgate, b_rgate, w_igate, b_igate,
                     lru_lambda, b_mi, b_mf, g_mhead, w_out, g_final):
    d_in = w_in.shape[-1]
    win = jnp.pad(w_in.astype(bf16), ((0, 0), (0, 0), (0, D_IN_PAD - d_in)))
    wif = w_in[:, :, C_IF:C_IF + 2 * H_B]
    wk = w_in[:, :, C_K:C_K + W_B].reshape(DEPTH, D_MODEL, H_B, DK)
    bias8 = jnp.concatenate([b_mi, b_mf], axis=-1)[:, None, :]
    return {
        "g_norm": g_norm[:, None, :],
        "win": win,
        "wift": jnp.swapaxes(wif, 1, 2).astype(bf16),
        "wkt": jnp.transpose(wk, (0, 2, 3, 1)).astype(bf16),
        "bias8": bias8,
        "brow": jnp.swapaxes(bias8, 1, 2),
        "conv_w": conv_w, "conv_b": conv_b[:, None, :],
        "wr": _block_diag_groups(w_rgate), "wi": _block_diag_groups(w_igate),
        "b_r": b_rgate[:, None, :], "b_i": b_igate[:, None, :], "lam": lru_lambda[:, None, :],
        "g_mhead": g_mhead[:, None, :],
        "wout": w_out.astype(bf16),
        "g_final": g_final[None, :],
    }


def kernel(x_prompt, x_sample, state_rglru_h, state_rglru_conv, state_mlstm_C, state_mlstm_n, state_mlstm_m, g_norm, w_in, conv_w, conv_b, w_rgate, b_rgate, w_igate, b_igate, lru_lambda, b_mi, b_mf, g_mhead, w_out, g_final):
    W = _prepare_weights(g_norm, w_in, conv_w, conv_b, w_rgate, b_rgate, w_igate, b_igate,
                         lru_lambda, b_mi, b_mf, g_mhead, w_out, g_final)

    x = x_prompt
    p_states = None
    for l in range(DEPTH):
        x, *p_states = _prompt_layer(x, W, l, p_states)
    p_h, p_conv, p_c, p_n, p_m = p_states

    nb = x_sample.shape[0]
    hist = CONV_W - 1
    xs = x_sample.reshape(nb, D_MODEL)
    conv0 = state_rglru_conv.reshape(DEPTH, nb, hist * W_A)
    m0 = state_mlstm_m.reshape(DEPTH, nb, 1, H_B)
    a_states, b_states = None, None
    for l in range(DEPTH):
        u = _dec_inproj(xs, W, l)
        s_h, s_conv, ma = _dec_rglru(u, state_rglru_h, conv0, W, l, a_states)
        a_states = (s_h, s_conv)
        s_c, s_n, s_m, mb = _dec_mlstm(u.reshape(nb, 1, D_IN_PAD), state_mlstm_C, state_mlstm_n, m0, W, l, b_states)
        b_states = (s_c, s_n, s_m)
        xs = _dec_out(xs, ma, mb.reshape(nb, W_B), W, l)

    return (x, xs.reshape(nb, 1, D_MODEL),
            p_h[:, :, 0, :], p_conv, p_c, p_n, p_m[:, :, 0, :H_B],
            s_h, s_conv.reshape(DEPTH, nb, hist, W_A), s_c, s_n, s_m.reshape(DEPTH, nb, H_B))
```
